```python
import math
import jax, jax.numpy as jnp
from jax import lax
import numpy as np

D_MODEL = 1024
BATCH = 8
SEQ = 2048
DEPTH = 4
DEC_BATCH = 128
DEC_SEQ = 1
PAST_LEN = 16384
PAGE_SIZE = 128

D_RNN = D_MODEL
RG_BLOCKS = 16
RG_BLOCK = D_RNN // RG_BLOCKS
RG_C = 8.0
CONV_W = 4
GDN_HEADS = 8
GDN_DK = 128
GDN_DV = 128
GDN_QK = GDN_HEADS * GDN_DK
GDN_VW = GDN_HEADS * GDN_DV
GDN_CONV_CH = 2 * GDN_QK + GDN_VW
CHUNK = 64
D_FF = 2816
ALPHA = (2 * DEPTH) ** 0.25
BETA_INIT = (8 * DEPTH) ** -0.25
LN_EPS = 1e-5
RMS_EPS = 1e-6
SPLITS = (D_RNN, D_RNN, GDN_QK, GDN_QK, GDN_VW, GDN_VW, GDN_HEADS, GDN_HEADS, D_MODEL, D_MODEL)
D_IN = 2 * D_RNN + 2 * GDN_QK + 2 * GDN_VW + 2 * GDN_HEADS + 2 * D_MODEL

kernel_name = "hybrid_rglru_gdn_macaron_deepnorm_step"


def layer_norm(x, g, b):
    xf = x.astype(jnp.float32)
    mu = jnp.mean(xf, -1, keepdims=True)
    var = jnp.mean(jnp.square(xf - mu), -1, keepdims=True)
    return ((xf - mu) * lax.rsqrt(var + LN_EPS) * g.astype(jnp.float32) + b.astype(jnp.float32)).astype(x.dtype)


def swiglu(x, w1, w3, w2):
    return (jax.nn.silu(x @ w1) * (x @ w3)) @ w2


def causal_conv(x, buf, w):
    T = x.shape[1]
    xx = jnp.concatenate([buf.astype(x.dtype), x], axis=1)
    y = xx[:, 0:T] * w[0]
    for j in range(1, CONV_W):
        y = y + xx[:, j:j + T] * w[j]
    return y, xx[:, -(CONV_W - 1):]


def rg_lru(x, h0, w_r, b_r, w_i, b_i, lam):
    B, T, _ = x.shape
    xb = x.reshape(B, T, RG_BLOCKS, RG_BLOCK)
    r = jax.nn.sigmoid(jnp.einsum('btnc,ncd->btnd', xb, w_r).reshape(B, T, D_RNN) + b_r).astype(jnp.float32)
    i = jax.nn.sigmoid(jnp.einsum('btnc,ncd->btnd', xb, w_i).reshape(B, T, D_RNN) + b_i)
    log_a = -RG_C * r * jax.nn.softplus(-lam.astype(jnp.float32))
    a = jnp.exp(log_a)
    bterm = jnp.sqrt(-jnp.expm1(2.0 * log_a)) * (i * x).astype(jnp.float32)
    bterm = bterm.at[:, 0].add(a[:, 0] * h0.astype(jnp.float32))

    def comb(e1, e2):
        a1, b1 = e1
        a2, b2 = e2
        return a1 * a2, a2 * b1 + b2

    _, h = lax.associative_scan(comb, (a, bterm), axis=1)
    return h.astype(x.dtype), h[:, -1]


def gated_delta_chunked(q, k, v, g, beta, S0):
    B, T, H, _ = q.shape
    C = min(CHUNK, T)
    N = -(-T // C)
    pad = N * C - T

    def prep(t):
        t = jnp.pad(t, [(0, 0), (0, pad)] + [(0, 0)] * (t.ndim - 2))
        t = jnp.moveaxis(t, 2, 1)
        return t.reshape((B, H, N, C) + t.shape[3:])

    q, k, v, g, beta = prep(q), prep(k), prep(v), prep(g), prep(beta)
    gc = jnp.cumsum(g, axis=-1)
    causal = jnp.tril(jnp.ones((C, C), bool))
    strict = jnp.tril(jnp.ones((C, C), bool), -1)
    diff = gc[..., :, None] - gc[..., None, :]
    decay = jnp.where(causal, jnp.exp(jnp.where(causal, diff, 0.0)), 0.0)
    kb = k * beta[..., None]
    vb = v * beta[..., None]
    L = jnp.where(strict, jnp.einsum('bhncd,bhnsd->bhncs', kb, k) * decay, 0.0)
    eye = jnp.eye(C, dtype=jnp.float32)
    rhs = jnp.concatenate([vb, kb * jnp.exp(gc)[..., None]], axis=-1)
    sol = lax.linalg.triangular_solve(eye + L, rhs, left_side=True, lower=True)
    u, w = sol[..., :GDN_DV], sol[..., GDN_DV:]
    a_intra = jnp.where(causal, jnp.einsum('bhncd,bhnsd->bhncs', q, k) * decay, 0.0)
    g_last = gc[..., -1]
    k_tail = k * jnp.exp(g_last[..., None] - gc)[..., None]
    q_dec = q * jnp.exp(gc)[..., None]
    xs = tuple(jnp.moveaxis(t, 2, 0) for t in (q_dec, k_tail, u, w, a_intra, g_last))

    def step(S, inp):
        qd, kt, ui, wi, ai, gl = inp
        v_new = ui - jnp.einsum('bhck,bhkv->bhcv', wi, S)
        o = jnp.einsum('bhck,bhkv->bhcv', qd, S) + jnp.einsum('bhcs,bhsv->bhcv', ai, v_new)
        S = S * jnp.exp(gl)[..., None, None] + jnp.einsum('bhck,bhcv->bhkv', kt, v_new)
        return S, o

    S, o = lax.scan(step, S0, xs)
    o = jnp.moveaxis(o, 0, 2).reshape(B, H, N * C, GDN_DV)[:, :, :T]
    return jnp.moveaxis(o, 1, 2), S


def l2norm(t):
    return t * lax.rsqrt(jnp.sum(jnp.square(t), -1, keepdims=True) + RMS_EPS)


def hybrid_mixer(x, h0, rc0, S0, gc0, w_in, b_in, rg_conv_w, rg_conv_b, rg_wr, rg_br, rg_wi, rg_bi,
                 rg_lambda, gdn_conv_w, gdn_a_log, gdn_dt_bias, gdn_norm_w, w_o):
    B, T, _ = x.shape
    proj = x @ w_in + b_in
    idx = [int(s) for s in np.cumsum(SPLITS)[:-1]]
    rg_x, rg_y, q, k, v, z, gdn_a, gdn_b, gate_a, gate_b = jnp.split(proj, idx, axis=-1)
    xc, rc_new = causal_conv(rg_x, rc0, rg_conv_w)
    xc = xc + rg_conv_b
    hseq, h_last = rg_lru(xc, h0, rg_wr, rg_br, rg_wi, rg_bi, rg_lambda)
    out_a = hseq * jax.nn.gelu(rg_y)
    qkv, gc_new = causal_conv(jnp.concatenate([q, k, v], axis=-1), gc0, gdn_conv_w)
    qkv = jax.nn.silu(qkv)
    q, k, v = jnp.split(qkv, [GDN_QK, 2 * GDN_QK], axis=-1)
    q = l2norm(q.reshape(B, T, GDN_HEADS, GDN_DK).astype(jnp.float32)) * (GDN_DK ** -0.5)
    k = l2norm(k.reshape(B, T, GDN_HEADS, GDN_DK).astype(jnp.float32))
    v = v.reshape(B, T, GDN_HEADS, GDN_DV).astype(jnp.float32)
    beta = jax.nn.sigmoid(gdn_b.astype(jnp.float32))
    g = -jnp.exp(gdn_a_log.astype(jnp.float32)) * jax.nn.softplus(
        gdn_a.astype(jnp.float32) + gdn_dt_bias.astype(jnp.float32))
    o, S_new = gated_delta_chunked(q, k, v, g, beta, S0.astype(jnp.float32))
    o = o * lax.rsqrt(jnp.mean(jnp.square(o), -1, keepdims=True) + RMS_EPS) * gdn_norm_w.astype(jnp.float32)
    out_b = o.reshape(B, T, GDN_VW).astype(x.dtype) * jax.nn.silu(z)
    merged = jax.nn.sigmoid(gate_a) * out_a + jax.nn.sigmoid(gate_b) * out_b
    new_state = (h_last.astype(h0.dtype), rc_new.astype(rc0.dtype), S_new.astype(S0.dtype), gc_new.astype(gc0.dtype))
    return merged @ w_o, new_state


def decoder_layer(x, h0, rc0, S0, gc0, ln1_g, ln1_b, ffn1_w1, ffn1_w3, ffn1_w2, w_in, b_in,
                  rg_conv_w, rg_conv_b, rg_wr, rg_br, rg_wi, rg_bi, rg_lambda, gdn_conv_w, gdn_a_log,
                  gdn_dt_bias, gdn_norm_w, w_o, ln2_g, ln2_b, ffn2_w1, ffn2_w3, ffn2_w2, ln3_g, ln3_b):
    x = layer_norm(ALPHA * x + 0.5 * swiglu(x, ffn1_w1, ffn1_w3, ffn1_w2), ln1_g, ln1_b)
    m, st = hybrid_mixer(x, h0, rc0, S0, gc0, w_in, b_in, rg_conv_w, rg_conv_b, rg_wr, rg_br, rg_wi,
                         rg_bi, rg_lambda, gdn_conv_w, gdn_a_log, gdn_dt_bias, gdn_norm_w, w_o)
    x = layer_norm(ALPHA * x + m, ln2_g, ln2_b)
    x = layer_norm(ALPHA * x + 0.5 * swiglu(x, ffn2_w1, ffn2_w3, ffn2_w2), ln3_g, ln3_b)
    return x, st


def trunk(x, h0, rc0, S0, gc0, params):
    hs, rcs, Ss, gcs = [], [], [], []
    for l in range(DEPTH):
        x, (h, rc, S, gc) = decoder_layer(x, h0[l], rc0[l], S0[l], gc0[l], *[p[l] for p in params])
        hs.append(h)
        rcs.append(rc)
        Ss.append(S)
        gcs.append(gc)
    return x, jnp.stack(hs), jnp.stack(rcs), jnp.stack(Ss), jnp.stack(gcs)


def setup_inputs(seed: int = 0) -> dict:
    key = jax.random.key(seed)
    ks = iter(jax.random.split(key, 48))

    def nrm(shape, scale):
        return jax.random.normal(next(ks), shape, jnp.float32) * scale

    L, D = DEPTH, D_MODEL
    x_prompt = nrm((BATCH, SEQ, D), 1.0)
    x_sample = nrm((DEC_BATCH, DEC_SEQ, D), 1.0)
    state_rglru_h = nrm((L, DEC_BATCH, D_RNN), 0.5)
    state_rglru_conv = nrm((L, DEC_BATCH, CONV_W - 1, D_RNN), 1.0)
    state_gdn_S = nrm((L, DEC_BATCH, GDN_HEADS, GDN_DK, GDN_DV), 0.1)
    state_gdn_conv = nrm((L, DEC_BATCH, CONV_W - 1, GDN_CONV_CH), 1.0)

    ln1_g = 1.0 + nrm((L, D), 0.02)
    ln1_b = nrm((L, D), 0.02)
    ffn1_w1 = nrm((L, D, D_FF), D ** -0.5)
    ffn1_w3 = nrm((L, D, D_FF), D ** -0.5)
    ffn1_w2 = nrm((L, D_FF, D), BETA_INIT * D_FF ** -0.5)
    w_in = nrm((L, D, D_IN), D ** -0.5)
    v_start = 2 * D_RNN + 2 * GDN_QK
    w_in = w_in.at[:, :, v_start:v_start + GDN_VW].multiply(BETA_INIT)
    b_in = nrm((L, D_IN), 0.01)
    rg_conv_w = nrm((L, CONV_W, D_RNN), CONV_W ** -0.5)
    rg_conv_b = nrm((L, D_RNN), 0.01)
    rg_wr = nrm((L, RG_BLOCKS, RG_BLOCK, RG_BLOCK), RG_BLOCK ** -0.5)
    rg_br = nrm((L, D_RNN), 0.01)
    rg_wi = nrm((L, RG_BLOCKS, RG_BLOCK, RG_BLOCK), RG_BLOCK ** -0.5)
    rg_bi = nrm((L, D_RNN), 0.01)
    u = jax.random.uniform(next(ks), (L, D_RNN), jnp.float32, 0.9, 0.999)
    a0 = u ** (1.0 / RG_C)
    rg_lambda = jnp.log(a0) - jnp.log1p(-a0)
    gdn_conv_w = nrm((L, CONV_W, GDN_CONV_CH), CONV_W ** -0.5)
    gdn_a_log = jnp.log(jax.random.uniform(next(ks), (L, GDN_HEADS), jnp.float32, 1.0, 16.0))
    dt = jnp.exp(jax.random.uniform(next(ks), (L, GDN_HEADS), jnp.float32, math.log(1e-3), math.log(1e-1)))
    gdn_dt_bias = dt + jnp.log(-jnp.expm1(-dt))
    gdn_norm_w = 1.0 + nrm((L, GDN_DV), 0.02)
    w_o = nrm((L, D, D), BETA_INIT * D ** -0.5)
    ln2_g = 1.0 + nrm((L, D), 0.02)
    ln2_b = nrm((L, D), 0.02)
    ffn2_w1 = nrm((L, D, D_FF), D ** -0.5)
    ffn2_w3 = nrm((L, D, D_FF), D ** -0.5)
    ffn2_w2 = nrm((L, D_FF, D), BETA_INIT * D_FF ** -0.5)
    ln3_g = 1.0 + nrm((L, D), 0.02)
    ln3_b = nrm((L, D), 0.02)
    return {"x_prompt": x_prompt, "x_sample": x_sample,
            "state_rglru_h": state_rglru_h, "state_rglru_conv": state_rglru_conv,
            "state_gdn_S": state_gdn_S, "state_gdn_conv": state_gdn_conv,
            "ln1_g": ln1_g, "ln1_b": ln1_b, "ffn1_w1": ffn1_w1, "ffn1_w3": ffn1_w3, "ffn1_w2": ffn1_w2,
            "w_in": w_in, "b_in": b_in, "rg_conv_w": rg_conv_w, "rg_conv_b": rg_conv_b,
            "rg_wr": rg_wr, "rg_br": rg_br, "rg_wi": rg_wi, "rg_bi": rg_bi, "rg_lambda": rg_lambda,
            "gdn_conv_w": gdn_conv_w, "gdn_a_log": gdn_a_log, "gdn_dt_bias": gdn_dt_bias,
            "gdn_norm_w": gdn_norm_w, "w_o": w_o, "ln2_g": ln2_g, "ln2_b": ln2_b,
            "ffn2_w1": ffn2_w1, "ffn2_w3": ffn2_w3, "ffn2_w2": ffn2_w2, "ln3_g": ln3_g, "ln3_b": ln3_b}


def reference(x_prompt, x_sample, state_rglru_h, state_rglru_conv, state_gdn_S, state_gdn_conv,
              ln1_g, ln1_b, ffn1_w1, ffn1_w3, ffn1_w2, w_in, b_in, rg_conv_w, rg_conv_b, rg_wr, rg_br,
              rg_wi, rg_bi, rg_lambda, gdn_conv_w, gdn_a_log, gdn_dt_bias, gdn_norm_w, w_o, ln2_g, ln2_b,
              ffn2_w1, ffn2_w3, ffn2_w2, ln3_g, ln3_b):
    params = (ln1_g, ln1_b, ffn1_w1, ffn1_w3, ffn1_w2, w_in, b_in, rg_conv_w, rg_conv_b, rg_wr, rg_br,
              rg_wi, rg_bi, rg_lambda, gdn_conv_w, gdn_a_log, gdn_dt_bias, gdn_norm_w, w_o, ln2_g, ln2_b,
              ffn2_w1, ffn2_w3, ffn2_w2, ln3_g, ln3_b)
    Bp = x_prompt.shape[0]
    dt = x_prompt.dtype
    y_prompt, p_h, p_rc, p_S, p_gc = trunk(
        x_prompt,
        jnp.zeros((DEPTH, Bp, D_RNN), dt),
        jnp.zeros((DEPTH, Bp, CONV_W - 1, D_RNN), dt),
        jnp.zeros((DEPTH, Bp, GDN_HEADS, GDN_DK, GDN_DV), dt),
        jnp.zeros((DEPTH, Bp, CONV_W - 1, GDN_CONV_CH), dt),
        params)
    y_sample, s_h, s_rc, s_S, s_gc = trunk(
        x_sample, state_rglru_h, state_rglru_conv, state_gdn_S, state_gdn_conv, params)
    return (y_prompt, y_sample, p_h, p_rc, p_S, p_gc, s_h, s_rc, s_S, s_gc)
```

```python
import functools

import jax
import jax.numpy as jnp
from jax import lax
from jax.experimental import pallas as pl
from jax.experimental.pallas import tpu as pltpu

F32 = jnp.float32
BF16 = jnp.bfloat16

D_MODEL = 1024
DEPTH = 4
ALPHA = (2 * DEPTH) ** 0.25
D_RNN = D_MODEL
RG_BLOCK = 64
RG_C = 8.0
CONV_W = 4
GDN_HEADS = 8
GDN_DK = 128
GDN_DV = 128
GDN_QK = GDN_HEADS * GDN_DK
GDN_VW = GDN_HEADS * GDN_DV
D_FF = 2816
LN_EPS = 1e-5
RMS_EPS = 1e-6
LANES = 128
SUBLANES = 8
N_GROUPS = D_RNN // LANES

OFF_RGX = 0
OFF_Q = D_RNN
OFF_K = OFF_Q + GDN_QK
OFF_V = OFF_K + GDN_QK
CONV_CH = OFF_V + GDN_VW
OFF_RGY = CONV_CH
OFF_Z = OFF_RGY + D_RNN
OFF_GA = OFF_Z + GDN_VW
OFF_GB = OFF_GA + D_MODEL
D_MAIN = OFF_GB + D_MODEL
TAIL = SUBLANES

VMEM_LIMIT = 56 * 1024 * 1024


def _mm(a, b):
    return jnp.dot(a.astype(BF16), b.astype(BF16), preferred_element_type=F32)


def _mm_nt(a, b):
    return lax.dot_general(a.astype(BF16), b.astype(BF16), (((1,), (1,)), ((), ())),
                           preferred_element_type=F32)


def _mm_tn(a, b):
    return lax.dot_general(a.astype(BF16), b.astype(BF16), (((0,), (0,)), ((), ())),
                           preferred_element_type=F32)


def _softplus(x):
    return jnp.maximum(x, 0.0) + jnp.log1p(jnp.exp(-jnp.abs(x)))


def _layer_norm(r, g, b):
    mu = jnp.mean(r, axis=-1, keepdims=True)
    c = r - mu
    var = jnp.mean(c * c, axis=-1, keepdims=True)
    return c * lax.rsqrt(var + LN_EPS) * g + b


def _ffn_ln_kernel(x_ref, w1_ref, w3_ref, w2_ref, g_ref, b_ref, o_ref, *, alpha, ff_chunk):
    x = x_ref[...]
    xb = x.astype(BF16)
    acc = None
    for c in range(D_FF // ff_chunk):
        sl = slice(c * ff_chunk, (c + 1) * ff_chunk)
        h1 = jnp.dot(xb, w1_ref[:, sl], preferred_element_type=F32)
        h3 = jnp.dot(xb, w3_ref[:, sl], preferred_element_type=F32)
        hh = (jax.nn.silu(h1) * h3).astype(BF16)
        y = jnp.dot(hh, w2_ref[sl, :], preferred_element_type=F32)
        acc = y if acc is None else acc + y
    o_ref[...] = _layer_norm(alpha * x + 0.5 * acc, g_ref[...], b_ref[...])


def _const_spec(shape):
    nd = len(shape)
    return pl.BlockSpec(shape, lambda *_: (0,) * nd, pipeline_mode=pl.Buffered(1))


def _ffn_ln(x, w1, w3, w2, g, b, *, alpha, tm):
    m = x.shape[0]
    return pl.pallas_call(
        functools.partial(_ffn_ln_kernel, alpha=alpha, ff_chunk=D_FF // 2),
        grid=(m // tm,),
        in_specs=[pl.BlockSpec((tm, D_MODEL), lambda i: (i, 0)),
                  _const_spec((D_MODEL, D_FF)), _const_spec((D_MODEL, D_FF)),
                  _const_spec((D_FF, D_MODEL)),
                  _const_spec((1, D_MODEL)), _const_spec((1, D_MODEL))],
        out_specs=pl.BlockSpec((tm, D_MODEL), lambda i: (i, 0)),
        out_shape=jax.ShapeDtypeStruct((m, D_MODEL), F32),
        compiler_params=pltpu.CompilerParams(dimension_semantics=("arbitrary",),
                                             vmem_limit_bytes=VMEM_LIMIT),
        name="ffn_ln",
    )(x, w1, w3, w2, g, b)


def _in_proj_kernel(x_ref, w_ref, b_ref, wab_ref, bab_ref, main_ref, ab_ref, *, n_chunk):
    xb = x_ref[...].astype(BF16)
    for c in range(D_MAIN // n_chunk):
        sl = slice(c * n_chunk, (c + 1) * n_chunk)
        main_ref[:, sl] = jnp.dot(xb, w_ref[:, sl], preferred_element_type=F32) + b_ref[:, sl]
    ab_ref[...] = jnp.dot(xb, wab_ref[...], preferred_element_type=F32) + bab_ref[...]


def _in_proj(x, w_main, b_main, w_ab, b_ab, *, tm):
    m = x.shape[0]
    return pl.pallas_call(
        functools.partial(_in_proj_kernel, n_chunk=1024),
        grid=(m // tm,),
        in_specs=[pl.BlockSpec((tm, D_MODEL), lambda i: (i, 0)),
                  _const_spec((D_MODEL, D_MAIN)), _const_spec((1, D_MAIN)),
                  _const_spec((D_MODEL, LANES)), _const_spec((1, LANES))],
        out_specs=[pl.BlockSpec((tm, D_MAIN), lambda i: (i, 0)),
                   pl.BlockSpec((tm, LANES), lambda i: (i, 0))],
        out_shape=[jax.ShapeDtypeStruct((m, D_MAIN), F32),
                   jax.ShapeDtypeStruct((m, LANES), F32)],
        compiler_params=pltpu.CompilerParams(dimension_semantics=("arbitrary",),
                                             vmem_limit_bytes=VMEM_LIMIT),
        name="in_proj",
    )(x, w_main, b_main, w_ab, b_ab)


def _out_proj_ln_kernel(x_ref, m_ref, w_ref, g_ref, b_ref, o_ref, *, alpha):
    y = jnp.dot(m_ref[...], w_ref[...], preferred_element_type=F32)
    o_ref[...] = _layer_norm(alpha * x_ref[...] + y, g_ref[...], b_ref[...])


def _out_proj_ln(x, merged, w_o, g, b, *, alpha, tm):
    m = x.shape[0]
    return pl.pallas_call(
        functools.partial(_out_proj_ln_kernel, alpha=alpha),
        grid=(m // tm,),
        in_specs=[pl.BlockSpec((tm, D_MODEL), lambda i: (i, 0)),
                  pl.BlockSpec((tm, D_MODEL), lambda i: (i, 0)),
                  _const_spec((D_MODEL, D_MODEL)),
                  _const_spec((1, D_MODEL)), _const_spec((1, D_MODEL))],
        out_specs=pl.BlockSpec((tm, D_MODEL), lambda i: (i, 0)),
        out_shape=jax.ShapeDtypeStruct((m, D_MODEL), F32),
        compiler_params=pltpu.CompilerParams(dimension_semantics=("arbitrary",),
                                             vmem_limit_bytes=VMEM_LIMIT),
        name="out_proj_ln",
    )(x, merged, w_o, g, b)


def _rg_gates(xc, wg, br, bi, c_lam):
    pre = _mm(xc, wg)
    r = jax.nn.sigmoid(pre[:, :LANES] + br)
    i = jax.nn.sigmoid(pre[:, LANES:] + bi)
    log_a = c_lam * r
    a = jnp.exp(log_a)
    bt = jnp.sqrt(1.0 - a * a) * (i * xc)
    return a, bt


def _l2norm(t):
    return t * lax.rsqrt(jnp.sum(t * t, axis=-1, keepdims=True) + RMS_EPS)


def _head_out(o, nw, z, hseq, rgy, ga, gb):
    o = o * lax.rsqrt(jnp.mean(o * o, axis=-1, keepdims=True) + RMS_EPS) * nw
    out_b = o * jax.nn.silu(z)
    out_a = hseq * jax.nn.gelu(rgy)
    return jax.nn.sigmoid(ga) * out_a + jax.nn.sigmoid(gb) * out_b


def _gdn_gates(ab, alog, dtb):
    g = -jnp.exp(alog) * _softplus(ab + dtb)
    beta = jax.nn.sigmoid(ab)
    return g, beta


def _tri_inverse(low, c):
    row = lax.broadcasted_iota(jnp.int32, (c, c), 0)
    col = lax.broadcasted_iota(jnp.int32, (c, c), 1)
    eye = (row == col).astype(F32)
    base = 16
    same = (row // base) == (col // base)
    n1 = -jnp.where(same, low, 0.0)
    p = eye + n1
    npow = n1
    step = 1
    while step * 2 < base:
        npow = _mm(npow, npow)
        p = p + _mm(p, npow)
        step *= 2
    blk = base
    while blk < c:
        off = ((row // (2 * blk)) == (col // (2 * blk))) & ((row // blk) != (col // blk))
        e = jnp.where(off, low, 0.0)
        p = p - _mm(_mm(p, e), p)
        blk *= 2
    return p


def _delta_chunk(qn, kn, v, gcol, grow, bcol, s):
    c = qn.shape[0]
    row = lax.broadcasted_iota(jnp.int32, (c, c), 0)
    col = lax.broadcasted_iota(jnp.int32, (c, c), 1)
    causal = row >= col
    strict = row > col
    decay = jnp.where(causal, jnp.exp(jnp.where(causal, gcol - grow, 0.0)), 0.0)
    kb = kn * bcol
    low = jnp.where(strict, _mm_nt(kb, kn) * decay, 0.0)
    a_intra = jnp.where(causal, _mm_nt(qn, kn) * decay, 0.0)
    tinv = _tri_inverse(low, c)
    egc = jnp.exp(gcol)
    sol = _mm(tinv, jnp.concatenate([v * bcol, kb * egc], axis=1))
    u = sol[:, :GDN_DV]
    w = sol[:, GDN_DV:]
    g_last = gcol[c - 1:c, :]
    k_tail = kn * jnp.exp(g_last - gcol)
    q_dec = qn * egc
    ws = _mm(jnp.concatenate([w, q_dec], axis=0), s)
    v_new = u - ws[:c]
    o = ws[c:] + _mm(a_intra, v_new)
    s_new = s * jnp.exp(g_last) + _mm_tn(k_tail, v_new)
    return o, s_new


def _mixer_prompt_kernel(main_ref, ab_ref, h0_ref, rc0_ref, s0_ref, gc0_ref,
                         cw_ref, cb_ref, wg_ref, br_ref, bi_ref, lam_ref,
                         alog_ref, dtb_ref, nw_ref,
                         merged_ref, h_ref, rc_ref, s_ref, gc_ref,
                         cbuf, h_sc, s_sc, *, tt, chunk):
    t = pl.program_id(1)
    nt = pl.num_programs(1)

    @pl.when(t == 0)
    def _():
        cbuf[0:TAIL, :] = jnp.zeros((TAIL, CONV_CH), F32)
        cbuf[TAIL - (CONV_W - 1):TAIL, 0:D_RNN] = rc0_ref[0]
        cbuf[TAIL - (CONV_W - 1):TAIL, D_RNN:CONV_CH] = gc0_ref[0]
        h_sc[...] = h0_ref[0]
        s_sc[...] = s0_ref[0]

    cbuf[TAIL:TAIL + tt, :] = main_ref[0, :, 0:CONV_CH]

    def conv(off):
        acc = None
        for i in range(CONV_W):
            r0 = TAIL - (CONV_W - 1) + i
            term = cw_ref[i:i + 1, off:off + LANES] * cbuf[r0:r0 + tt, off:off + LANES]
            acc = term if acc is None else acc + term
        return acc

    row = lax.broadcasted_iota(jnp.int32, (tt, LANES), 0)

    g, beta = _gdn_gates(ab_ref[0], alog_ref[...], dtb_ref[...])
    rin = row % chunk
    d = 1
    while d < chunk:
        g = g + jnp.where(rin >= d, pltpu.roll(g, d, axis=0), 0.0)
        d *= 2
    g_t = g.T

    for j in range(N_GROUPS):
        lo = j * LANES
        xc = conv(OFF_RGX + lo) + cb_ref[:, lo:lo + LANES]
        c_lam = -RG_C * _softplus(-lam_ref[:, lo:lo + LANES])
        a, bt = _rg_gates(xc, wg_ref[j], br_ref[:, lo:lo + LANES], bi_ref[:, lo:lo + LANES], c_lam)
        d = 1
        while d < tt:
            keep = row >= d
            a_sh = jnp.where(keep, pltpu.roll(a, d, axis=0), 1.0)
            b_sh = jnp.where(keep, pltpu.roll(bt, d, axis=0), 0.0)
            bt = a * b_sh + bt
            a = a * a_sh
            d *= 2
        hseq = bt + a * h_sc[:, lo:lo + LANES]
        h_sc[:, lo:lo + LANES] = hseq[tt - 1:tt, :]
        qn = _l2norm(jax.nn.silu(conv(OFF_Q + lo))) * (GDN_DK ** -0.5)
        kn = _l2norm(jax.nn.silu(conv(OFF_K + lo)))
        v = jax.nn.silu(conv(OFF_V + lo))
        s = s_sc[j]
        outs = []
        for n in range(tt // chunk):
            r0 = n * chunk
            o, s = _delta_chunk(qn[r0:r0 + chunk], kn[r0:r0 + chunk], v[r0:r0 + chunk],
                                g[r0:r0 + chunk, j:j + 1], g_t[j:j + 1, r0:r0 + chunk],
                                beta[r0:r0 + chunk, GDN_HEADS + j:GDN_HEADS + j + 1], s)
            outs.append(o)
        s_sc[j] = s
        o = outs[0] if len(outs) == 1 else jnp.concatenate(outs, axis=0)
        merged = _head_out(o, nw_ref[...], main_ref[0, :, OFF_Z + lo:OFF_Z + lo + LANES],
                           hseq, main_ref[0, :, OFF_RGY + lo:OFF_RGY + lo + LANES],
                           main_ref[0, :, OFF_GA + lo:OFF_GA + lo + LANES],
                           main_ref[0, :, OFF_GB + lo:OFF_GB + lo + LANES])
        merged_ref[0, :, lo:lo + LANES] = merged.astype(BF16)

    cbuf[0:TAIL, :] = cbuf[tt:tt + TAIL, :]

    @pl.when(t == nt - 1)
    def _():
        h_ref[0] = h_sc[...]
        rc_ref[0] = cbuf[TAIL - (CONV_W - 1):TAIL, 0:D_RNN]
        gc_ref[0] = cbuf[TAIL - (CONV_W - 1):TAIL, D_RNN:CONV_CH]
        s_ref[0] = s_sc[...]


def _mixer_prompt(main, ab, h0, rc0, s0, gc0, lp, *, tt, chunk):
    bsz, seq, _ = main.shape
    nt = seq // tt
    per_b3 = lambda b, t: (b, 0, 0)
    per_b4 = lambda b, t: (b, 0, 0, 0)
    kern = functools.partial(_mixer_prompt_kernel, tt=tt, chunk=chunk)
    return pl.pallas_call(
        kern,
        grid=(bsz, nt),
        in_specs=[pl.BlockSpec((1, tt, D_MAIN), lambda b, t: (b, t, 0)),
                  pl.BlockSpec((1, tt, LANES), lambda b, t: (b, t, 0)),
                  pl.BlockSpec((1, 1, D_RNN), per_b3),
                  pl.BlockSpec((1, CONV_W - 1, D_RNN), per_b3),
                  pl.BlockSpec((1, GDN_HEADS, GDN_DK, GDN_DV), per_b4),
                  pl.BlockSpec((1, CONV_W - 1, 3 * GDN_QK), per_b3),
                  _const_spec((CONV_W, CONV_CH)), _const_spec((1, D_RNN)),
                  _const_spec((N_GROUPS, LANES, 2 * LANES)),
                  _const_spec((1, D_RNN)), _const_spec((1, D_RNN)), _const_spec((1, D_RNN)),
                  _const_spec((1, LANES)), _const_spec((1, LANES)), _const_spec((1, GDN_DV))],
        out_specs=[pl.BlockSpec((1, tt, D_MODEL), lambda b, t: (b, t, 0)),
                   pl.BlockSpec((1, 1, D_RNN), per_b3),
                   pl.BlockSpec((1, CONV_W - 1, D_RNN), per_b3),
                   pl.BlockSpec((1, GDN_HEADS, GDN_DK, GDN_DV), per_b4),
                   pl.BlockSpec((1, CONV_W - 1, 3 * GDN_QK), per_b3)],
        out_shape=[jax.ShapeDtypeStruct((bsz, seq, D_MODEL), BF16),
                   jax.ShapeDtypeStruct((bsz, 1, D_RNN), F32),
                   jax.ShapeDtypeStruct((bsz, CONV_W - 1, D_RNN), F32),
                   jax.ShapeDtypeStruct((bsz, GDN_HEADS, GDN_DK, GDN_DV), F32),
                   jax.ShapeDtypeStruct((bsz, CONV_W - 1, 3 * GDN_QK), F32)],
        scratch_shapes=[pltpu.VMEM((tt + TAIL, CONV_CH), F32),
                        pltpu.VMEM((1, D_RNN), F32),
                        pltpu.VMEM((GDN_HEADS, GDN_DK, GDN_DV), F32)],
        compiler_params=pltpu.CompilerParams(dimension_semantics=("arbitrary", "arbitrary"),
                                             vmem_limit_bytes=VMEM_LIMIT),
        name="mixer_prompt",
    )(main, ab, h0, rc0, s0, gc0, lp["cw"], lp["cb"], lp["wg"], lp["br"], lp["bi"], lp["lam"],
      lp["alog"], lp["dtb"], lp["nw"])


def _mixer_sample_kernel(main_ref, ab_ref, h0_ref, rc0_ref, s0_ref, gc0_ref,
                         cw_ref, cb_ref, wg_ref, br_ref, bi_ref, lam_ref,
                         alog_ref, dtb_ref, nw_ref,
                         merged_ref, h_ref, rc_ref, s_ref, gc_ref, o_sc, *, bb):
    nprev = CONV_W - 1

    def conv(hist_ref, ch, off_hist, off_main):
        acc = cw_ref[nprev:CONV_W, off_main:off_main + LANES] * main_ref[:, off_main:off_main + LANES]
        for i in range(nprev):
            acc = acc + (cw_ref[i:i + 1, off_main:off_main + LANES]
                         * hist_ref[:, i * ch + off_hist:i * ch + off_hist + LANES])
        return acc

    rc_ref[:, 0:2 * D_RNN] = rc0_ref[:, D_RNN:3 * D_RNN]
    rc_ref[:, 2 * D_RNN:3 * D_RNN] = main_ref[:, 0:D_RNN]
    qkv_w = 3 * GDN_QK
    gc_ref[:, 0:2 * qkv_w] = gc0_ref[:, qkv_w:3 * qkv_w]
    gc_ref[:, 2 * qkv_w:3 * qkv_w] = main_ref[:, D_RNN:CONV_CH]

    g, beta = _gdn_gates(ab_ref[...], alog_ref[...], dtb_ref[...])
    eg = jnp.exp(g)

    for j in range(N_GROUPS):
        lo = j * LANES
        xc = conv(rc0_ref, D_RNN, lo, OFF_RGX + lo) + cb_ref[:, lo:lo + LANES]
        c_lam = -RG_C * _softplus(-lam_ref[:, lo:lo + LANES])
        a, bt = _rg_gates(xc, wg_ref[j], br_ref[:, lo:lo + LANES], bi_ref[:, lo:lo + LANES], c_lam)
        hseq = a * h0_ref[:, lo:lo + LANES] + bt
        h_ref[:, lo:lo + LANES] = hseq

        qn = _l2norm(jax.nn.silu(conv(gc0_ref, qkv_w, lo, OFF_Q + lo))) * (GDN_DK ** -0.5)
        kn = _l2norm(jax.nn.silu(conv(gc0_ref, qkv_w, GDN_QK + lo, OFF_K + lo)))
        v = jax.nn.silu(conv(gc0_ref, qkv_w, 2 * GDN_QK + lo, OFF_V + lo))
        qn_t = qn.T
        kn_t = kn.T
        for b in range(bb):
            s = s0_ref[b, j]
            kcol = kn_t[:, b:b + 1]
            qcol = qn_t[:, b:b + 1]
            egb = eg[b:b + 1, j:j + 1]
            ks = jnp.sum(s * kcol, axis=0, keepdims=True)
            v_new = beta[b:b + 1, GDN_HEADS + j:GDN_HEADS + j + 1] * (v[b:b + 1, :] - egb * ks)
            s_new = s * egb + kcol * v_new
            s_ref[b, j] = s_new
            o_sc[b:b + 1, :] = jnp.sum(s_new * qcol, axis=0, keepdims=True)
        merged = _head_out(o_sc[...], nw_ref[...], main_ref[:, OFF_Z + lo:OFF_Z + lo + LANES],
                           hseq, main_ref[:, OFF_RGY + lo:OFF_RGY + lo + LANES],
                           main_ref[:, OFF_GA + lo:OFF_GA + lo + LANES],
                           main_ref[:, OFF_GB + lo:OFF_GB + lo + LANES])
        merged_ref[:, lo:lo + LANES] = merged.astype(BF16)


def _mixer_sample(main, ab, h0, rc0, s0, gc0, lp, *, bb):
    bsz = main.shape[0]
    row2 = lambda i: (i, 0)
    kern = functools.partial(_mixer_sample_kernel, bb=bb)
    qkv3 = 3 * 3 * GDN_QK
    return pl.pallas_call(
        kern,
        grid=(bsz // bb,),
        in_specs=[pl.BlockSpec((bb, D_MAIN), row2),
                  pl.BlockSpec((bb, LANES), row2),
                  pl.BlockSpec((bb, D_RNN), row2),
                  pl.BlockSpec((bb, 3 * D_RNN), row2),
                  pl.BlockSpec((bb, GDN_HEADS, GDN_DK, GDN_DV), lambda i: (i, 0, 0, 0)),
                  pl.BlockSpec((bb, qkv3), row2),
                  _const_spec((CONV_W, CONV_CH)), _const_spec((1, D_RNN)),
                  _const_spec((N_GROUPS, LANES, 2 * LANES)),
                  _const_spec((1, D_RNN)), _const_spec((1, D_RNN)), _const_spec((1, D_RNN)),
                  _const_spec((1, LANES)), _const_spec((1, LANES)), _const_spec((1, GDN_DV))],
        out_specs=[pl.BlockSpec((bb, D_MODEL), row2),
                   pl.BlockSpec((bb, D_RNN), row2),
                   pl.BlockSpec((bb, 3 * D_RNN), row2),
                   pl.BlockSpec((bb, GDN_HEADS, GDN_DK, GDN_DV), lambda i: (i, 0, 0, 0)),
                   pl.BlockSpec((bb, qkv3), row2)],
        out_shape=[jax.ShapeDtypeStruct((bsz, D_MODEL), BF16),
                   jax.ShapeDtypeStruct((bsz, D_RNN), F32),
                   jax.ShapeDtypeStruct((bsz, 3 * D_RNN), F32),
                   jax.ShapeDtypeStruct((bsz, GDN_HEADS, GDN_DK, GDN_DV), F32),
                   jax.ShapeDtypeStruct((bsz, qkv3), F32)],
        scratch_shapes=[pltpu.VMEM((bb, GDN_DV), F32)],
        compiler_params=pltpu.CompilerParams(dimension_semantics=("arbitrary",),
                                             vmem_limit_bytes=VMEM_LIMIT),
        name="mixer_sample",
    )(main, ab, h0, rc0, s0, gc0, lp["cw"], lp["cb"], lp["wg"], lp["br"], lp["bi"], lp["lam"],
      lp["alog"], lp["dtb"], lp["nw"])


def _prep_layer(p):
    (ln1_g, ln1_b, ffn1_w1, ffn1_w3, ffn1_w2, w_in, b_in, rg_conv_w, rg_conv_b, rg_wr, rg_br,
     rg_wi, rg_bi, rg_lambda, gdn_conv_w, gdn_a_log, gdn_dt_bias, gdn_norm_w, w_o, ln2_g, ln2_b,
     ffn2_w1, ffn2_w3, ffn2_w2, ln3_g, ln3_b) = p
    row = lambda v: v.reshape(1, -1)

    def reorder(w):
        a0 = 2 * D_RNN + 2 * GDN_QK + 2 * GDN_VW
        main = jnp.concatenate([w[..., 0:D_RNN], w[..., 2 * D_RNN:2 * D_RNN + 3 * GDN_QK],
                                w[..., D_RNN:2 * D_RNN], w[..., 2 * D_RNN + 3 * GDN_QK:a0],
                                w[..., a0 + 2 * GDN_HEADS:]], axis=-1)
        ab = w[..., a0:a0 + 2 * GDN_HEADS]
        ab = jnp.pad(ab, [(0, 0)] * (w.ndim - 1) + [(0, LANES - 2 * GDN_HEADS)])
        return main, ab

    w_main, w_ab = reorder(w_in)
    b_main, b_ab = reorder(row(b_in))

    def pair_blocks(w):
        w = w.reshape(N_GROUPS, 2, RG_BLOCK, RG_BLOCK)
        z = jnp.zeros((N_GROUPS, RG_BLOCK, RG_BLOCK), w.dtype)
        top = jnp.concatenate([w[:, 0], z], axis=2)
        bot = jnp.concatenate([z, w[:, 1]], axis=2)
        return jnp.concatenate([top, bot], axis=1)

    wg = jnp.concatenate([pair_blocks(rg_wr), pair_blocks(rg_wi)], axis=2).astype(BF16)
    pad_h = lambda v: jnp.pad(row(v), [(0, 0), (0, LANES - GDN_HEADS)])
    return dict(
        ln1_g=row(ln1_g), ln1_b=row(ln1_b), ln2_g=row(ln2_g), ln2_b=row(ln2_b),
        ln3_g=row(ln3_g), ln3_b=row(ln3_b),
        f1=(ffn1_w1.astype(BF16), ffn1_w3.astype(BF16), ffn1_w2.astype(BF16)),
        f2=(ffn2_w1.astype(BF16), ffn2_w3.astype(BF16), ffn2_w2.astype(BF16)),
        w_main=w_main.astype(BF16), b_main=b_main, w_ab=w_ab.astype(BF16), b_ab=b_ab,
        w_o=w_o.astype(BF16),
        cw=jnp.concatenate([rg_conv_w, gdn_conv_w], axis=1), cb=row(rg_conv_b),
        wg=wg, br=row(rg_br), bi=row(rg_bi), lam=row(rg_lambda),
        alog=pad_h(gdn_a_log), dtb=pad_h(gdn_dt_bias), nw=row(gdn_norm_w))


def _trunk(x, h0, rc0, s0, gc0, layers, *, alpha, tm, prompt, tt=128, chunk=64, bb=8):
    bsz, seq, _ = x.shape
    m = bsz * seq
    xf = x.reshape(m, D_MODEL)
    hs, rcs, ss, gcs = [], [], [], []
    for l, lp in enumerate(layers):
        xf = _ffn_ln(xf, *lp["f1"], lp["ln1_g"], lp["ln1_b"], alpha=alpha, tm=tm)
        main, ab = _in_proj(xf, lp["w_main"], lp["b_main"], lp["w_ab"], lp["b_ab"], tm=min(tm, 256))
        if prompt:
            merged, h, rc, s, gc = _mixer_prompt(
                main.reshape(bsz, seq, D_MAIN), ab.reshape(bsz, seq, LANES),
                h0[l].reshape(bsz, 1, D_RNN), rc0[l], s0[l], gc0[l], lp, tt=tt, chunk=chunk)
            h = h.reshape(bsz, D_RNN)
        else:
            merged, h, rc, s, gc = _mixer_sample(
                main, ab, h0[l], rc0[l].reshape(bsz, -1), s0[l], gc0[l].reshape(bsz, -1), lp, bb=bb)
            rc = rc.reshape(bsz, CONV_W - 1, D_RNN)
            gc = gc.reshape(bsz, CONV_W - 1, 3 * GDN_QK)
        xf = _out_proj_ln(xf, merged.reshape(m, D_MODEL), lp["w_o"], lp["ln2_g"], lp["ln2_b"],
                          alpha=alpha, tm=tm)
        xf = _ffn_ln(xf, *lp["f2"], lp["ln3_g"], lp["ln3_b"], alpha=alpha, tm=tm)
        hs.append(h)
        rcs.append(rc)
        ss.append(s)
        gcs.append(gc)
    return (xf.reshape(bsz, seq, D_MODEL), jnp.stack(hs), jnp.stack(rcs), jnp.stack(ss),
            jnp.stack(gcs))


def kernel(x_prompt, x_sample, state_rglru_h, state_rglru_conv, state_gdn_S, state_gdn_conv, ln1_g, ln1_b, ffn1_w1, ffn1_w3, ffn1_w2, w_in, b_in, rg_conv_w, rg_conv_b, rg_wr, rg_br, rg_wi, rg_bi, rg_lambda, gdn_conv_w, gdn_a_log, gdn_dt_bias, gdn_norm_w, w_o, ln2_g, ln2_b, ffn2_w1, ffn2_w3, ffn2_w2, ln3_g, ln3_b):
    params = (ln1_g, ln1_b, ffn1_w1, ffn1_w3, ffn1_w2, w_in, b_in, rg_conv_w, rg_conv_b, rg_wr, rg_br,
              rg_wi, rg_bi, rg_lambda, gdn_conv_w, gdn_a_log, gdn_dt_bias, gdn_norm_w, w_o, ln2_g, ln2_b,
              ffn2_w1, ffn2_w3, ffn2_w2, ln3_g, ln3_b)
    depth = ln1_g.shape[0]
    alpha = ALPHA
    layers = [_prep_layer([p[l] for p in params]) for l in range(depth)]
    bp = x_prompt.shape[0]
    dt = x_prompt.dtype
    zeros = lambda *s: jnp.zeros((depth, bp) + s, dt)
    y_p, p_h, p_rc, p_s, p_gc = _trunk(
        x_prompt, zeros(D_RNN), zeros(CONV_W - 1, D_RNN), zeros(GDN_HEADS, GDN_DK, GDN_DV),
        zeros(CONV_W - 1, 3 * GDN_QK), layers, alpha=alpha, tm=512, prompt=True)
    y_s, s_h, s_rc, s_s, s_gc = _trunk(
        x_sample, state_rglru_h, state_rglru_conv, state_gdn_S, state_gdn_conv, layers,
        alpha=alpha, tm=x_sample.shape[0] * x_sample.shape[1], prompt=False)
    return (y_p, y_s, p_h, p_rc, p_s, p_gc, s_h, s_rc, s_s, s_gc)
```

```python
import functools

import jax
import jax.numpy as jnp
from jax import lax
from jax.experimental import pallas as pl
from jax.experimental.pallas import tpu as pltpu

F32 = jnp.float32
BF16 = jnp.bfloat16

D_MODEL = 1024
DEPTH = 4
ALPHA = (2 * DEPTH) ** 0.25
D_RNN = D_MODEL
RG_BLOCK = 64
RG_C = 8.0
CONV_W = 4
GDN_HEADS = 8
GDN_DK = 128
GDN_DV = 128
GDN_QK = GDN_HEADS * GDN_DK
GDN_VW = GDN_HEADS * GDN_DV
D_FF = 2816
LN_EPS = 1e-5
RMS_EPS = 1e-6
LANES = 128
SUBLANES = 8
N_GROUPS = D_RNN // LANES

OFF_RGX = 0
OFF_Q = D_RNN
OFF_K = OFF_Q + GDN_QK
OFF_V = OFF_K + GDN_QK
CONV_CH = OFF_V + GDN_VW
OFF_RGY = CONV_CH
OFF_Z = OFF_RGY + D_RNN
OFF_GA = OFF_Z + GDN_VW
OFF_GB = OFF_GA + D_MODEL
D_MAIN = OFF_GB + D_MODEL
TAIL = SUBLANES
INV_BASE = 16

VMEM_LIMIT = 56 * 1024 * 1024


def _mm(a, b):
    return jnp.dot(a.astype(BF16), b.astype(BF16), preferred_element_type=F32)


def _mm_nt(a, b):
    return lax.dot_general(a.astype(BF16), b.astype(BF16), (((1,), (1,)), ((), ())),
                           preferred_element_type=F32)


def _mm_tn(a, b):
    return lax.dot_general(a.astype(BF16), b.astype(BF16), (((0,), (0,)), ((), ())),
                           preferred_element_type=F32)


def _softplus(x):
    return jnp.maximum(x, 0.0) + jnp.log1p(jnp.exp(-jnp.abs(x)))


def _layer_norm(r, g, b):
    mu = jnp.mean(r, axis=-1, keepdims=True)
    c = r - mu
    var = jnp.mean(c * c, axis=-1, keepdims=True)
    return c * lax.rsqrt(var + LN_EPS) * g + b


def _ffn_ln_kernel(x_ref, w1_ref, w3_ref, w2_ref, g_ref, b_ref, o_ref, *, alpha, ff_chunk):
    x = x_ref[...]
    xb = x.astype(BF16)
    acc = None
    for c in range(D_FF // ff_chunk):
        sl = slice(c * ff_chunk, (c + 1) * ff_chunk)
        h1 = jnp.dot(xb, w1_ref[:, sl], preferred_element_type=F32)
        h3 = jnp.dot(xb, w3_ref[:, sl], preferred_element_type=F32)
        hh = (jax.nn.silu(h1) * h3).astype(BF16)
        y = jnp.dot(hh, w2_ref[sl, :], preferred_element_type=F32)
        acc = y if acc is None else acc + y
    o_ref[...] = _layer_norm(alpha * x + 0.5 * acc, g_ref[...], b_ref[...])


def _const_spec(shape):
    nd = len(shape)
    return pl.BlockSpec(shape, lambda *_: (0,) * nd, pipeline_mode=pl.Buffered(1))


def _ffn_ln(x, w1, w3, w2, g, b, *, alpha, tm):
    m = x.shape[0]
    return pl.pallas_call(
        functools.partial(_ffn_ln_kernel, alpha=alpha, ff_chunk=D_FF // 2),
        grid=(m // tm,),
        in_specs=[pl.BlockSpec((tm, D_MODEL), lambda i: (i, 0)),
                  _const_spec((D_MODEL, D_FF)), _const_spec((D_MODEL, D_FF)),
                  _const_spec((D_FF, D_MODEL)),
                  _const_spec((1, D_MODEL)), _const_spec((1, D_MODEL))],
        out_specs=pl.BlockSpec((tm, D_MODEL), lambda i: (i, 0)),
        out_shape=jax.ShapeDtypeStruct((m, D_MODEL), F32),
        compiler_params=pltpu.CompilerParams(dimension_semantics=("arbitrary",),
                                             vmem_limit_bytes=VMEM_LIMIT),
        name="ffn_ln",
    )(x, w1, w3, w2, g, b)


def _in_proj_kernel(x_ref, w_ref, b_ref, wab_ref, bab_ref, main_ref, ab_ref, *, n_chunk):
    xb = x_ref[...].astype(BF16)
    for c in range(D_MAIN // n_chunk):
        sl = slice(c * n_chunk, (c + 1) * n_chunk)
        main_ref[:, sl] = jnp.dot(xb, w_ref[:, sl], preferred_element_type=F32) + b_ref[:, sl]
    ab_ref[...] = jnp.dot(xb, wab_ref[...], preferred_element_type=F32) + bab_ref[...]


def _in_proj(x, w_main, b_main, w_ab, b_ab, *, tm):
    m = x.shape[0]
    return pl.pallas_call(
        functools.partial(_in_proj_kernel, n_chunk=1024),
        grid=(m // tm,),
        in_specs=[pl.BlockSpec((tm, D_MODEL), lambda i: (i, 0)),
                  _const_spec((D_MODEL, D_MAIN)), _const_spec((1, D_MAIN)),
                  _const_spec((D_MODEL, LANES)), _const_spec((1, LANES))],
        out_specs=[pl.BlockSpec((tm, D_MAIN), lambda i: (i, 0)),
                   pl.BlockSpec((tm, LANES), lambda i: (i, 0))],
        out_shape=[jax.ShapeDtypeStruct((m, D_MAIN), F32),
                   jax.ShapeDtypeStruct((m, LANES), F32)],
        compiler_params=pltpu.CompilerParams(dimension_semantics=("arbitrary",),
                                             vmem_limit_bytes=VMEM_LIMIT),
        name="in_proj",
    )(x, w_main, b_main, w_ab, b_ab)


def _out_proj_ln_kernel(x_ref, m_ref, w_ref, g_ref, b_ref, o_ref, *, alpha):
    y = jnp.dot(m_ref[...], w_ref[...], preferred_element_type=F32)
    o_ref[...] = _layer_norm(alpha * x_ref[...] + y, g_ref[...], b_ref[...])


def _out_proj_ln(x, merged, w_o, g, b, *, alpha, tm):
    m = x.shape[0]
    return pl.pallas_call(
        functools.partial(_out_proj_ln_kernel, alpha=alpha),
        grid=(m // tm,),
        in_specs=[pl.BlockSpec((tm, D_MODEL), lambda i: (i, 0)),
                  pl.BlockSpec((tm, D_MODEL), lambda i: (i, 0)),
                  _const_spec((D_MODEL, D_MODEL)),
                  _const_spec((1, D_MODEL)), _const_spec((1, D_MODEL))],
        out_specs=pl.BlockSpec((tm, D_MODEL), lambda i: (i, 0)),
        out_shape=jax.ShapeDtypeStruct((m, D_MODEL), F32),
        compiler_params=pltpu.CompilerParams(dimension_semantics=("arbitrary",),
                                             vmem_limit_bytes=VMEM_LIMIT),
        name="out_proj_ln",
    )(x, merged, w_o, g, b)


def _rg_gates(xc, wg, br, bi, c_lam):
    pre = _mm(xc, wg)
    r = jax.nn.sigmoid(pre[:, :LANES] + br)
    i = jax.nn.sigmoid(pre[:, LANES:] + bi)
    log_a = c_lam * r
    a = jnp.exp(log_a)
    bt = jnp.sqrt(1.0 - a * a) * (i * xc)
    return a, bt


def _l2norm(t):
    return t * lax.rsqrt(jnp.sum(t * t, axis=-1, keepdims=True) + RMS_EPS)


def _head_out(o, nw, z, hseq, rgy, ga, gb):
    o = o * lax.rsqrt(jnp.mean(o * o, axis=-1, keepdims=True) + RMS_EPS) * nw
    out_b = o * jax.nn.silu(z)
    out_a = hseq * jax.nn.gelu(rgy)
    return jax.nn.sigmoid(ga) * out_a + jax.nn.sigmoid(gb) * out_b


def _gdn_gates(ab, alog, dtb):
    g = -jnp.exp(alog) * _softplus(ab + dtb)
    beta = jax.nn.sigmoid(ab)
    return g, beta


def _rg_scan(a, b, h_prev):
    n = a.shape[0]
    nv = n // SUBLANES
    a3 = a.reshape(nv, SUBLANES, LANES)
    b3 = b.reshape(nv, SUBLANES, LANES)
    sub = lax.broadcasted_iota(jnp.int32, (nv, SUBLANES, LANES), 1)
    d = 1
    while d < SUBLANES:
        keep = sub >= d
        a_sh = jnp.where(keep, pltpu.roll(a3, d, axis=1), 1.0)
        b_sh = jnp.where(keep, pltpu.roll(b3, d, axis=1), 0.0)
        b3 = a3 * b_sh + b3
        a3 = a3 * a_sh
        d *= 2
    carry = h_prev
    hs = []
    for v in range(nv):
        hv = a3[v] * carry + b3[v]
        hs.append(hv)
        carry = hv[SUBLANES - 1:SUBLANES, :]
    return jnp.concatenate(hs, axis=0), carry


def _tri_inverse_many(lows, c):
    row = lax.broadcasted_iota(jnp.int32, (c, c), 0)
    col = lax.broadcasted_iota(jnp.int32, (c, c), 1)
    eye = (row == col).astype(F32)
    same = (row // INV_BASE) == (col // INV_BASE)
    npows = [(-jnp.where(same, low, 0.0)).astype(BF16) for low in lows]
    ps = [eye - jnp.where(same, low, 0.0) for low in lows]
    step = 1
    while step * 2 < INV_BASE:
        sq = [jnp.dot(x, x, preferred_element_type=F32) for x in npows]
        npows = [x.astype(BF16) for x in sq]
        ps = [p + jnp.dot(p.astype(BF16), x, preferred_element_type=F32) for p, x in zip(ps, npows)]
        step *= 2
    blk = INV_BASE
    while blk < c:
        off = ((row // (2 * blk)) == (col // (2 * blk))) & ((row // blk) != (col // blk))
        es = [jnp.where(off, low, 0.0).astype(BF16) for low in lows]
        pbs = [p.astype(BF16) for p in ps]
        pes = [jnp.dot(pb, e, preferred_element_type=F32).astype(BF16) for pb, e in zip(pbs, es)]
        ps = [p - jnp.dot(pe, pb, preferred_element_type=F32) for p, pe, pb in zip(ps, pes, pbs)]
        blk *= 2
    return ps


def _mixer_prompt_kernel(x_ref, h0_ref, rc0_ref, s0_ref, gc0_ref,
                         wm_ref, bm_ref, wab_ref, bab_ref,
                         cw_ref, cb_ref, wg_ref, br_ref, bi_ref, lam_ref,
                         alog_ref, dtb_ref, nw_ref, wo_ref, g2_ref, b2_ref,
                         y_ref, h_ref, rc_ref, s_ref, gc_ref,
                         cbuf, pbuf, h_sc, s_sc, *, tt, chunk, n_chunk, alpha):
    t = pl.program_id(1)
    nt = pl.num_programs(1)
    nc = tt // chunk
    hist = TAIL - (CONV_W - 1)

    @pl.when(t == 0)
    def _():
        cbuf[0:TAIL, :] = jnp.zeros((TAIL, CONV_CH), F32)
        cbuf[hist:TAIL, 0:D_RNN] = rc0_ref[0]
        cbuf[hist:TAIL, D_RNN:CONV_CH] = gc0_ref[0]
        h_sc[...] = h0_ref[0]
        s_sc[...] = s0_ref[0]

    x = x_ref[0]
    xb = x.astype(BF16)
    for c0 in range(0, D_MAIN, n_chunk):
        res = (jnp.dot(xb, wm_ref[:, c0:c0 + n_chunk], preferred_element_type=F32)
               + bm_ref[:, c0:c0 + n_chunk])
        if c0 < CONV_CH:
            cbuf[TAIL:TAIL + tt, c0:c0 + n_chunk] = res
        else:
            pbuf[:, c0 - CONV_CH:c0 - CONV_CH + n_chunk] = res
    ab = jnp.dot(xb, wab_ref[...], preferred_element_type=F32) + bab_ref[...]

    def conv(off):
        acc = None
        for i in range(CONV_W):
            term = cw_ref[i:i + 1, off:off + LANES] * cbuf[hist + i:hist + i + tt, off:off + LANES]
            acc = term if acc is None else acc + term
        return acc

    def rest(off, lo):
        return pbuf[:, off - CONV_CH + lo:off - CONV_CH + lo + LANES]

    row = lax.broadcasted_iota(jnp.int32, (tt, LANES), 0)
    g, beta = _gdn_gates(ab, alog_ref[...], dtb_ref[...])
    rin = row % chunk
    d = 1
    while d < chunk:
        g = g + jnp.where(rin >= d, pltpu.roll(g, d, axis=0), 0.0)
        d *= 2
    g_t = g.T
    eg = jnp.exp(g)

    gated_a, qs, ks, vs = [], [], [], []
    for j in range(N_GROUPS):
        lo = j * LANES
        xc = conv(OFF_RGX + lo) + cb_ref[:, lo:lo + LANES]
        c_lam = -RG_C * _softplus(-lam_ref[:, lo:lo + LANES])
        a, bt = _rg_gates(xc, wg_ref[j], br_ref[:, lo:lo + LANES], bi_ref[:, lo:lo + LANES], c_lam)
        hseq, h_last = _rg_scan(a, bt, h_sc[:, lo:lo + LANES])
        h_sc[:, lo:lo + LANES] = h_last
        gated_a.append(jax.nn.sigmoid(rest(OFF_GA, lo)) * (hseq * jax.nn.gelu(rest(OFF_RGY, lo))))
        qs.append(_l2norm(jax.nn.silu(conv(OFF_Q + lo))) * (GDN_DK ** -0.5))
        ks.append(_l2norm(jax.nn.silu(conv(OFF_K + lo))))
        vs.append(jax.nn.silu(conv(OFF_V + lo)))

    cbuf[0:TAIL, :] = cbuf[tt:tt + TAIL, :]

    rw = lax.broadcasted_iota(jnp.int32, (chunk, chunk), 0)
    cl = lax.broadcasted_iota(jnp.int32, (chunk, chunk), 1)
    causal = rw >= cl
    strict = rw > cl
    probs = [(j, n) for j in range(N_GROUPS) for n in range(nc)]
    sl = lambda arr, n: arr[n * chunk:(n + 1) * chunk]
    gcol = {p: sl(g, p[1])[:, p[0]:p[0] + 1] for p in probs}
    egc = {p: sl(eg, p[1])[:, p[0]:p[0] + 1] for p in probs}
    bcol = {p: sl(beta, p[1])[:, GDN_HEADS + p[0]:GDN_HEADS + p[0] + 1] for p in probs}
    decay = {p: jnp.exp(jnp.where(causal, gcol[p] - g_t[p[0]:p[0] + 1, p[1] * chunk:(p[1] + 1) * chunk], 0.0))
             for p in probs}
    kn = {p: sl(ks[p[0]], p[1]) for p in probs}
    qn = {p: sl(qs[p[0]], p[1]) for p in probs}
    kb = {p: kn[p] * bcol[p] for p in probs}
    knb = {p: kn[p].astype(BF16) for p in probs}
    kk = {p: lax.dot_general(kb[p].astype(BF16), knb[p], (((1,), (1,)), ((), ())),
                             preferred_element_type=F32) for p in probs}
    qk = {p: lax.dot_general(qn[p].astype(BF16), knb[p], (((1,), (1,)), ((), ())),
                             preferred_element_type=F32) for p in probs}
    lows = [jnp.where(strict, kk[p] * decay[p], 0.0) for p in probs]
    a_intra = {p: jnp.where(causal, qk[p] * decay[p], 0.0).astype(BF16) for p in probs}
    tinv = dict(zip(probs, _tri_inverse_many(lows, chunk)))
    sol = {p: jnp.dot(tinv[p].astype(BF16),
                      jnp.concatenate([sl(vs[p[0]], p[1]) * bcol[p], kb[p] * egc[p]], axis=1).astype(BF16),
                      preferred_element_type=F32) for p in probs}
    g_last = {p: gcol[p][chunk - 1:chunk, :] for p in probs}
    wq = {p: jnp.concatenate([sol[p][:, GDN_DV:], qn[p] * egc[p]], axis=0).astype(BF16) for p in probs}
    k_tail = {p: (kn[p] * jnp.exp(g_last[p] - gcol[p])).astype(BF16) for p in probs}

    states = [s_sc[j] for j in range(N_GROUPS)]
    outs = {}
    for n in range(nc):
        ws = [jnp.dot(wq[(j, n)], states[j].astype(BF16), preferred_element_type=F32)
              for j in range(N_GROUPS)]
        v_new = [(sol[(j, n)][:, :GDN_DV] - ws[j][:chunk]).astype(BF16) for j in range(N_GROUPS)]
        for j in range(N_GROUPS):
            outs[(j, n)] = ws[j][chunk:] + jnp.dot(a_intra[(j, n)], v_new[j], preferred_element_type=F32)
        states = [states[j] * jnp.exp(g_last[(j, n)])
                  + lax.dot_general(k_tail[(j, n)], v_new[j], (((0,), (0,)), ((), ())),
                                    preferred_element_type=F32) for j in range(N_GROUPS)]
    for j in range(N_GROUPS):
        s_sc[j] = states[j]

    merged = []
    for j in range(N_GROUPS):
        lo = j * LANES
        o = outs[(j, 0)] if nc == 1 else jnp.concatenate([outs[(j, n)] for n in range(nc)], axis=0)
        o = o * lax.rsqrt(jnp.mean(o * o, axis=-1, keepdims=True) + RMS_EPS) * nw_ref[...]
        out_b = o * jax.nn.silu(rest(OFF_Z, lo))
        merged.append((gated_a[j] + jax.nn.sigmoid(rest(OFF_GB, lo)) * out_b).astype(BF16))
    m = jnp.concatenate(merged, axis=1)
    y = jnp.dot(m, wo_ref[...], preferred_element_type=F32)
    y_ref[0] = _layer_norm(alpha * x + y, g2_ref[...], b2_ref[...])

    @pl.when(t == nt - 1)
    def _():
        h_ref[0] = h_sc[...]
        rc_ref[0] = cbuf[hist:TAIL, 0:D_RNN]
        gc_ref[0] = cbuf[hist:TAIL, D_RNN:CONV_CH]
        s_ref[0] = s_sc[...]


def _mixer_prompt(x, h0, rc0, s0, gc0, lp, *, alpha, tt, chunk):
    bsz, seq, _ = x.shape
    nt = seq // tt
    per_b3 = lambda b, t: (b, 0, 0)
    per_b4 = lambda b, t: (b, 0, 0, 0)
    kern = functools.partial(_mixer_prompt_kernel, tt=tt, chunk=chunk, n_chunk=512, alpha=alpha)
    return pl.pallas_call(
        kern,
        grid=(bsz, nt),
        in_specs=[pl.BlockSpec((1, tt, D_MODEL), lambda b, t: (b, t, 0)),
                  pl.BlockSpec((1, 1, D_RNN), per_b3),
                  pl.BlockSpec((1, CONV_W - 1, D_RNN), per_b3),
                  pl.BlockSpec((1, GDN_HEADS, GDN_DK, GDN_DV), per_b4),
                  pl.BlockSpec((1, CONV_W - 1, 3 * GDN_QK), per_b3),
                  _const_spec((D_MODEL, D_MAIN)), _const_spec((1, D_MAIN)),
                  _const_spec((D_MODEL, LANES)), _const_spec((1, LANES)),
                  _const_spec((CONV_W, CONV_CH)), _const_spec((1, D_RNN)),
                  _const_spec((N_GROUPS, LANES, 2 * LANES)),
                  _const_spec((1, D_RNN)), _const_spec((1, D_RNN)), _const_spec((1, D_RNN)),
                  _const_spec((1, LANES)), _const_spec((1, LANES)), _const_spec((1, GDN_DV)),
                  _const_spec((D_MODEL, D_MODEL)), _const_spec((1, D_MODEL)), _const_spec((1, D_MODEL))],
        out_specs=[pl.BlockSpec((1, tt, D_MODEL), lambda b, t: (b, t, 0)),
                   pl.BlockSpec((1, 1, D_RNN), per_b3),
                   pl.BlockSpec((1, CONV_W - 1, D_RNN), per_b3),
                   pl.BlockSpec((1, GDN_HEADS, GDN_DK, GDN_DV), per_b4),
                   pl.BlockSpec((1, CONV_W - 1, 3 * GDN_QK), per_b3)],
        out_shape=[jax.ShapeDtypeStruct((bsz, seq, D_MODEL), F32),
                   jax.ShapeDtypeStruct((bsz, 1, D_RNN), F32),
                   jax.ShapeDtypeStruct((bsz, CONV_W - 1, D_RNN), F32),
                   jax.ShapeDtypeStruct((bsz, GDN_HEADS, GDN_DK, GDN_DV), F32),
                   jax.ShapeDtypeStruct((bsz, CONV_W - 1, 3 * GDN_QK), F32)],
        scratch_shapes=[pltpu.VMEM((tt + TAIL, CONV_CH), F32),
                        pltpu.VMEM((tt, D_MAIN - CONV_CH), F32),
                        pltpu.VMEM((1, D_RNN), F32),
                        pltpu.VMEM((GDN_HEADS, GDN_DK, GDN_DV), F32)],
        compiler_params=pltpu.CompilerParams(dimension_semantics=("arbitrary", "arbitrary"),
                                             vmem_limit_bytes=VMEM_LIMIT),
        name="mixer_prompt",
    )(x, h0, rc0, s0, gc0, lp["w_main"], lp["b_main"], lp["w_ab"], lp["b_ab"],
      lp["cw"], lp["cb"], lp["wg"], lp["br"], lp["bi"], lp["lam"],
      lp["alog"], lp["dtb"], lp["nw"], lp["w_o"], lp["ln2_g"], lp["ln2_b"])


def _mixer_sample_kernel(main_ref, ab_ref, h0_ref, rc0_ref, s0_ref, gc0_ref,
                         cw_ref, cb_ref, wg_ref, br_ref, bi_ref, lam_ref,
                         alog_ref, dtb_ref, nw_ref,
                         merged_ref, h_ref, rc_ref, s_ref, gc_ref, o_sc, *, bb):
    nprev = CONV_W - 1

    def conv(hist_ref, ch, off_hist, off_main):
        acc = cw_ref[nprev:CONV_W, off_main:off_main + LANES] * main_ref[:, off_main:off_main + LANES]
        for i in range(nprev):
            acc = acc + (cw_ref[i:i + 1, off_main:off_main + LANES]
                         * hist_ref[:, i * ch + off_hist:i * ch + off_hist + LANES])
        return acc

    rc_ref[:, 0:2 * D_RNN] = rc0_ref[:, D_RNN:3 * D_RNN]
    rc_ref[:, 2 * D_RNN:3 * D_RNN] = main_ref[:, 0:D_RNN]
    qkv_w = 3 * GDN_QK
    gc_ref[:, 0:2 * qkv_w] = gc0_ref[:, qkv_w:3 * qkv_w]
    gc_ref[:, 2 * qkv_w:3 * qkv_w] = main_ref[:, D_RNN:CONV_CH]

    g, beta = _gdn_gates(ab_ref[...], alog_ref[...], dtb_ref[...])
    eg = jnp.exp(g)

    for j in range(N_GROUPS):
        lo = j * LANES
        xc = conv(rc0_ref, D_RNN, lo, OFF_RGX + lo) + cb_ref[:, lo:lo + LANES]
        c_lam = -RG_C * _softplus(-lam_ref[:, lo:lo + LANES])
        a, bt = _rg_gates(xc, wg_ref[j], br_ref[:, lo:lo + LANES], bi_ref[:, lo:lo + LANES], c_lam)
        hseq = a * h0_ref[:, lo:lo + LANES] + bt
        h_ref[:, lo:lo + LANES] = hseq

        qn = _l2norm(jax.nn.silu(conv(gc0_ref, qkv_w, lo, OFF_Q + lo))) * (GDN_DK ** -0.5)
        kn = _l2norm(jax.nn.silu(conv(gc0_ref, qkv_w, GDN_QK + lo, OFF_K + lo)))
        v = jax.nn.silu(conv(gc0_ref, qkv_w, 2 * GDN_QK + lo, OFF_V + lo))
        qn_t = qn.T
        kn_t = kn.T
        for b in range(bb):
            s = s0_ref[b, j]
            kcol = kn_t[:, b:b + 1]
            qcol = qn_t[:, b:b + 1]
            egb = eg[b:b + 1, j:j + 1]
            ks = jnp.sum(s * kcol, axis=0, keepdims=True)
            v_new = beta[b:b + 1, GDN_HEADS + j:GDN_HEADS + j + 1] * (v[b:b + 1, :] - egb * ks)
            s_new = s * egb + kcol * v_new
            s_ref[b, j] = s_new
            o_sc[b:b + 1, :] = jnp.sum(s_new * qcol, axis=0, keepdims=True)
        merged = _head_out(o_sc[...], nw_ref[...], main_ref[:, OFF_Z + lo:OFF_Z + lo + LANES],
                           hseq, main_ref[:, OFF_RGY + lo:OFF_RGY + lo + LANES],
                           main_ref[:, OFF_GA + lo:OFF_GA + lo + LANES],
                           main_ref[:, OFF_GB + lo:OFF_GB + lo + LANES])
        merged_ref[:, lo:lo + LANES] = merged.astype(BF16)


def _mixer_sample(main, ab, h0, rc0, s0, gc0, lp, *, bb):
    bsz = main.shape[0]
    row2 = lambda i: (i, 0)
    kern = functools.partial(_mixer_sample_kernel, bb=bb)
    qkv3 = 3 * 3 * GDN_QK
    return pl.pallas_call(
        kern,
        grid=(bsz // bb,),
        in_specs=[pl.BlockSpec((bb, D_MAIN), row2),
                  pl.BlockSpec((bb, LANES), row2),
                  pl.BlockSpec((bb, D_RNN), row2),
                  pl.BlockSpec((bb, 3 * D_RNN), row2),
                  pl.BlockSpec((bb, GDN_HEADS, GDN_DK, GDN_DV), lambda i: (i, 0, 0, 0)),
                  pl.BlockSpec((bb, qkv3), row2),
                  _const_spec((CONV_W, CONV_CH)), _const_spec((1, D_RNN)),
                  _const_spec((N_GROUPS, LANES, 2 * LANES)),
                  _const_spec((1, D_RNN)), _const_spec((1, D_RNN)), _const_spec((1, D_RNN)),
                  _const_spec((1, LANES)), _const_spec((1, LANES)), _const_spec((1, GDN_DV))],
        out_specs=[pl.BlockSpec((bb, D_MODEL), row2),
                   pl.BlockSpec((bb, D_RNN), row2),
                   pl.BlockSpec((bb, 3 * D_RNN), row2),
                   pl.BlockSpec((bb, GDN_HEADS, GDN_DK, GDN_DV), lambda i: (i, 0, 0, 0)),
                   pl.BlockSpec((bb, qkv3), row2)],
        out_shape=[jax.ShapeDtypeStruct((bsz, D_MODEL), BF16),
                   jax.ShapeDtypeStruct((bsz, D_RNN), F32),
                   jax.ShapeDtypeStruct((bsz, 3 * D_RNN), F32),
                   jax.ShapeDtypeStruct((bsz, GDN_HEADS, GDN_DK, GDN_DV), F32),
                   jax.ShapeDtypeStruct((bsz, qkv3), F32)],
        scratch_shapes=[pltpu.VMEM((bb, GDN_DV), F32)],
        compiler_params=pltpu.CompilerParams(dimension_semantics=("arbitrary",),
                                             vmem_limit_bytes=VMEM_LIMIT),
        name="mixer_sample",
    )(main, ab, h0, rc0, s0, gc0, lp["cw"], lp["cb"], lp["wg"], lp["br"], lp["bi"], lp["lam"],
      lp["alog"], lp["dtb"], lp["nw"])


def _prep_layer(p):
    (ln1_g, ln1_b, ffn1_w1, ffn1_w3, ffn1_w2, w_in, b_in, rg_conv_w, rg_conv_b, rg_wr, rg_br,
     rg_wi, rg_bi, rg_lambda, gdn_conv_w, gdn_a_log, gdn_dt_bias, gdn_norm_w, w_o, ln2_g, ln2_b,
     ffn2_w1, ffn2_w3, ffn2_w2, ln3_g, ln3_b) = p
    row = lambda v: v.reshape(1, -1)

    def reorder(w):
        a0 = 2 * D_RNN + 2 * GDN_QK + 2 * GDN_VW
        main = jnp.concatenate([w[..., 0:D_RNN], w[..., 2 * D_RNN:2 * D_RNN + 3 * GDN_QK],
                                w[..., D_RNN:2 * D_RNN], w[..., 2 * D_RNN + 3 * GDN_QK:a0],
                                w[..., a0 + 2 * GDN_HEADS:]], axis=-1)
        ab = w[..., a0:a0 + 2 * GDN_HEADS]
        ab = jnp.pad(ab, [(0, 0)] * (w.ndim - 1) + [(0, LANES - 2 * GDN_HEADS)])
        return main, ab

    w_main, w_ab = reorder(w_in)
    b_main, b_ab = reorder(row(b_in))

    def pair_blocks(w):
        w = w.reshape(N_GROUPS, 2, RG_BLOCK, RG_BLOCK)
        z = jnp.zeros((N_GROUPS, RG_BLOCK, RG_BLOCK), w.dtype)
        top = jnp.concatenate([w[:, 0], z], axis=2)
        bot = jnp.concatenate([z, w[:, 1]], axis=2)
        return jnp.concatenate([top, bot], axis=1)

    wg = jnp.concatenate([pair_blocks(rg_wr), pair_blocks(rg_wi)], axis=2).astype(BF16)
    pad_h = lambda v: jnp.pad(row(v), [(0, 0), (0, LANES - GDN_HEADS)])
    return dict(
        ln1_g=row(ln1_g), ln1_b=row(ln1_b), ln2_g=row(ln2_g), ln2_b=row(ln2_b),
        ln3_g=row(ln3_g), ln3_b=row(ln3_b),
        f1=(ffn1_w1.astype(BF16), ffn1_w3.astype(BF16), ffn1_w2.astype(BF16)),
        f2=(ffn2_w1.astype(BF16), ffn2_w3.astype(BF16), ffn2_w2.astype(BF16)),
        w_main=w_main.astype(BF16), b_main=b_main, w_ab=w_ab.astype(BF16), b_ab=b_ab,
        w_o=w_o.astype(BF16),
        cw=jnp.concatenate([rg_conv_w, gdn_conv_w], axis=1), cb=row(rg_conv_b),
        wg=wg, br=row(rg_br), bi=row(rg_bi), lam=row(rg_lambda),
        alog=pad_h(gdn_a_log), dtb=pad_h(gdn_dt_bias), nw=row(gdn_norm_w))


def _trunk(x, h0, rc0, s0, gc0, layers, *, alpha, tm, prompt, tt=256, chunk=64, bb=8):
    bsz, seq, _ = x.shape
    m = bsz * seq
    xf = x.reshape(m, D_MODEL)
    hs, rcs, ss, gcs = [], [], [], []
    for l, lp in enumerate(layers):
        xf = _ffn_ln(xf, *lp["f1"], lp["ln1_g"], lp["ln1_b"], alpha=alpha, tm=tm)
        if prompt:
            y, h, rc, s, gc = _mixer_prompt(
                xf.reshape(bsz, seq, D_MODEL), h0[l].reshape(bsz, 1, D_RNN), rc0[l], s0[l], gc0[l],
                lp, alpha=alpha, tt=min(tt, seq), chunk=chunk)
            xf = y.reshape(m, D_MODEL)
            h = h.reshape(bsz, D_RNN)
        else:
            main, ab = _in_proj(xf, lp["w_main"], lp["b_main"], lp["w_ab"], lp["b_ab"], tm=tm)
            merged, h, rc, s, gc = _mixer_sample(
                main, ab, h0[l], rc0[l].reshape(bsz, -1), s0[l], gc0[l].reshape(bsz, -1), lp, bb=bb)
            rc = rc.reshape(bsz, CONV_W - 1, D_RNN)
            gc = gc.reshape(bsz, CONV_W - 1, 3 * GDN_QK)
            xf = _out_proj_ln(xf, merged, lp["w_o"], lp["ln2_g"], lp["ln2_b"], alpha=alpha, tm=tm)
        xf = _ffn_ln(xf, *lp["f2"], lp["ln3_g"], lp["ln3_b"], alpha=alpha, tm=tm)
        hs.append(h)
        rcs.append(rc)
        ss.append(s)
        gcs.append(gc)
    return (xf.reshape(bsz, seq, D_MODEL), jnp.stack(hs), jnp.stack(rcs), jnp.stack(ss),
            jnp.stack(gcs))


def kernel(x_prompt, x_sample, state_rglru_h, state_rglru_conv, state_gdn_S, state_gdn_conv, ln1_g, ln1_b, ffn1_w1, ffn1_w3, ffn1_w2, w_in, b_in, rg_conv_w, rg_conv_b, rg_wr, rg_br, rg_wi, rg_bi, rg_lambda, gdn_conv_w, gdn_a_log, gdn_dt_bias, gdn_norm_w, w_o, ln2_g, ln2_b, ffn2_w1, ffn2_w3, ffn2_w2, ln3_g, ln3_b):
    params = (ln1_g, ln1_b, ffn1_w1, ffn1_w3, ffn1_w2, w_in, b_in, rg_conv_w, rg_conv_b, rg_wr, rg_br,
              rg_wi, rg_bi, rg_lambda, gdn_conv_w, gdn_a_log, gdn_dt_bias, gdn_norm_w, w_o, ln2_g, ln2_b,
              ffn2_w1, ffn2_w3, ffn2_w2, ln3_g, ln3_b)
    depth = ln1_g.shape[0]
    layers = [_prep_layer([p[l] for p in params]) for l in range(depth)]
    bp = x_prompt.shape[0]
    dt = x_prompt.dtype
    zeros = lambda *s: jnp.zeros((depth, bp) + s, dt)
    y_p, p_h, p_rc, p_s, p_gc = _trunk(
        x_prompt, zeros(D_RNN), zeros(CONV_W - 1, D_RNN), zeros(GDN_HEADS, GDN_DK, GDN_DV),
        zeros(CONV_W - 1, 3 * GDN_QK), layers, alpha=ALPHA, tm=512, prompt=True)
    y_s, s_h, s_rc, s_s, s_gc = _trunk(
        x_sample, state_rglru_h, state_rglru_conv, state_gdn_S, state_gdn_conv, layers,
        alpha=ALPHA, tm=x_sample.shape[0] * x_sample.shape[1], prompt=False)
    return (y_p, y_s, p_h, p_rc, p_s, p_gc, s_h, s_rc, s_s, s_gc)
```

```python
import functools

import jax
import jax.numpy as jnp
from jax import lax
from jax.experimental import pallas as pl
from jax.experimental.pallas import tpu as pltpu

F32 = jnp.float32
BF16 = jnp.bfloat16

D_MODEL = 1024
DEPTH = 4
ALPHA = (2 * DEPTH) ** 0.25
D_RNN = D_MODEL
RG_BLOCK = 64
RG_C = 8.0
CONV_W = 4
GDN_HEADS = 8
GDN_DK = 128
GDN_DV = 128
GDN_QK = GDN_HEADS * GDN_DK
GDN_VW = GDN_HEADS * GDN_DV
D_FF = 2816
LN_EPS = 1e-5
RMS_EPS = 1e-6
LANES = 128
SUBLANES = 8
N_GROUPS = D_RNN // LANES

P_RGX, P_Q, P_K, P_V, P_RGY, P_Z, P_GA, P_GB = (i * LANES for i in range(8))
GROUP_W = 8 * LANES
CONV_GW = 4 * LANES
CONV_CH = N_GROUPS * CONV_GW
D_MAIN = N_GROUPS * GROUP_W
TAIL = SUBLANES
HIST = TAIL - (CONV_W - 1)
INV_BASE = 16
FFN_STEP = 256

VMEM_LIMIT = 60 * 1024 * 1024


def _mm(a, b):
    return jnp.dot(a.astype(BF16), b.astype(BF16), preferred_element_type=F32)


def _softplus(x):
    return jnp.maximum(x, 0.0) + jnp.log1p(jnp.exp(-jnp.abs(x)))


def _layer_norm(r, g, b):
    mu = jnp.mean(r, axis=-1, keepdims=True)
    c = r - mu
    var = jnp.mean(c * c, axis=-1, keepdims=True)
    return c * lax.rsqrt(var + LN_EPS) * g + b


def _swiglu_steps(x, w1_ref, w3_ref, w2_ref, *, ff_chunk):
    xb = x.astype(BF16)
    acc = None
    pending = None
    for c in range(D_FF // ff_chunk):
        sl = slice(c * ff_chunk, (c + 1) * ff_chunk)
        h1 = jnp.dot(xb, w1_ref[:, sl], preferred_element_type=F32)
        h3 = jnp.dot(xb, w3_ref[:, sl], preferred_element_type=F32)
        if pending is not None:
            y = jnp.dot(pending[0], w2_ref[pending[1], :], preferred_element_type=F32)
            acc = y if acc is None else acc + y
        pending = ((jax.nn.silu(h1) * h3).astype(BF16), sl)
        if c + 1 < D_FF // ff_chunk:
            yield None
    y = jnp.dot(pending[0], w2_ref[pending[1], :], preferred_element_type=F32)
    yield y if acc is None else acc + y


def _ffn_ln_math(x, w1_ref, w3_ref, w2_ref, g, b, *, alpha, ff_chunk):
    acc = None
    for acc in _swiglu_steps(x, w1_ref, w3_ref, w2_ref, ff_chunk=ff_chunk):
        pass
    return _layer_norm(alpha * x + 0.5 * acc, g, b)


def _ffn_ln_kernel(x_ref, w1_ref, w3_ref, w2_ref, g_ref, b_ref, o_ref, *, alpha, ff_chunk):
    o_ref[...] = _ffn_ln_math(x_ref[...], w1_ref, w3_ref, w2_ref, g_ref[...], b_ref[...],
                              alpha=alpha, ff_chunk=ff_chunk)


def _const_spec(shape):
    nd = len(shape)
    return pl.BlockSpec(shape, lambda *_: (0,) * nd, pipeline_mode=pl.Buffered(1))


def _ffn_ln(x, w1, w3, w2, g, b, *, alpha, tm):
    m = x.shape[0]
    return pl.pallas_call(
        functools.partial(_ffn_ln_kernel, alpha=alpha, ff_chunk=D_FF // 2),
        grid=(m // tm,),
        in_specs=[pl.BlockSpec((tm, D_MODEL), lambda i: (i, 0)),
                  _const_spec((D_MODEL, D_FF)), _const_spec((D_MODEL, D_FF)),
                  _const_spec((D_FF, D_MODEL)),
                  _const_spec((1, D_MODEL)), _const_spec((1, D_MODEL))],
        out_specs=pl.BlockSpec((tm, D_MODEL), lambda i: (i, 0)),
        out_shape=jax.ShapeDtypeStruct((m, D_MODEL), F32),
        compiler_params=pltpu.CompilerParams(dimension_semantics=("arbitrary",),
                                             vmem_limit_bytes=VMEM_LIMIT),
        name="ffn_ln",
    )(x, w1, w3, w2, g, b)


def _in_proj_kernel(x_ref, w_ref, b_ref, wab_ref, bab_ref, main_ref, ab_ref, *, n_chunk):
    xb = x_ref[...].astype(BF16)
    for c in range(D_MAIN // n_chunk):
        sl = slice(c * n_chunk, (c + 1) * n_chunk)
        main_ref[:, sl] = jnp.dot(xb, w_ref[:, sl], preferred_element_type=F32) + b_ref[:, sl]
    ab_ref[...] = jnp.dot(xb, wab_ref[...], preferred_element_type=F32) + bab_ref[...]


def _in_proj(x, w_main, b_main, w_ab, b_ab, *, tm):
    m = x.shape[0]
    return pl.pallas_call(
        functools.partial(_in_proj_kernel, n_chunk=GROUP_W),
        grid=(m // tm,),
        in_specs=[pl.BlockSpec((tm, D_MODEL), lambda i: (i, 0)),
                  _const_spec((D_MODEL, D_MAIN)), _const_spec((1, D_MAIN)),
                  _const_spec((D_MODEL, LANES)), _const_spec((1, LANES))],
        out_specs=[pl.BlockSpec((tm, D_MAIN), lambda i: (i, 0)),
                   pl.BlockSpec((tm, LANES), lambda i: (i, 0))],
        out_shape=[jax.ShapeDtypeStruct((m, D_MAIN), F32),
                   jax.ShapeDtypeStruct((m, LANES), F32)],
        compiler_params=pltpu.CompilerParams(dimension_semantics=("arbitrary",),
                                             vmem_limit_bytes=VMEM_LIMIT),
        name="in_proj",
    )(x, w_main, b_main, w_ab, b_ab)


def _out_proj_ln_kernel(x_ref, m_ref, w_ref, g_ref, b_ref, o_ref, *, alpha):
    y = jnp.dot(m_ref[...], w_ref[...], preferred_element_type=F32)
    o_ref[...] = _layer_norm(alpha * x_ref[...] + y, g_ref[...], b_ref[...])


def _out_proj_ln(x, merged, w_o, g, b, *, alpha, tm):
    m = x.shape[0]
    return pl.pallas_call(
        functools.partial(_out_proj_ln_kernel, alpha=alpha),
        grid=(m // tm,),
        in_specs=[pl.BlockSpec((tm, D_MODEL), lambda i: (i, 0)),
                  pl.BlockSpec((tm, D_MODEL), lambda i: (i, 0)),
                  _const_spec((D_MODEL, D_MODEL)),
                  _const_spec((1, D_MODEL)), _const_spec((1, D_MODEL))],
        out_specs=pl.BlockSpec((tm, D_MODEL), lambda i: (i, 0)),
        out_shape=jax.ShapeDtypeStruct((m, D_MODEL), F32),
        compiler_params=pltpu.CompilerParams(dimension_semantics=("arbitrary",),
                                             vmem_limit_bytes=VMEM_LIMIT),
        name="out_proj_ln",
    )(x, merged, w_o, g, b)


def _rg_gates(xc, wg, br, bi, c_lam):
    pre = _mm(xc, wg)
    r = jax.nn.sigmoid(pre[:, :LANES] + br)
    i = jax.nn.sigmoid(pre[:, LANES:] + bi)
    log_a = c_lam * r
    a = jnp.exp(log_a)
    bt = jnp.sqrt(1.0 - a * a) * (i * xc)
    return a, bt


def _l2norm(t):
    return t * lax.rsqrt(jnp.sum(t * t, axis=-1, keepdims=True) + RMS_EPS)


def _rms_gate(o, nw, zg):
    return o * lax.rsqrt(jnp.mean(o * o, axis=-1, keepdims=True) + RMS_EPS) * nw * zg


def _gdn_gates(ab, alog, dtb):
    g = -jnp.exp(alog) * _softplus(ab + dtb)
    beta = jax.nn.sigmoid(ab)
    return g, beta


def _rg_scan(a, b, h_prev):
    n = a.shape[0]
    nv = n // SUBLANES
    a3 = a.reshape(nv, SUBLANES, LANES)
    b3 = b.reshape(nv, SUBLANES, LANES)
    sub = lax.broadcasted_iota(jnp.int32, (nv, SUBLANES, LANES), 1)
    d = 1
    while d < SUBLANES:
        keep = sub >= d
        a_sh = jnp.where(keep, pltpu.roll(a3, d, axis=1), 1.0)
        b_sh = jnp.where(keep, pltpu.roll(b3, d, axis=1), 0.0)
        b3 = a3 * b_sh + b3
        a3 = a3 * a_sh
        d *= 2
    carry = h_prev
    hs = []
    for v in range(nv):
        hv = a3[v] * carry + b3[v]
        hs.append(hv)
        carry = hv[SUBLANES - 1:SUBLANES, :]
    return jnp.concatenate(hs, axis=0), carry


def _tri_inverse_many(lows, c):
    row = lax.broadcasted_iota(jnp.int32, (c, c), 0)
    col = lax.broadcasted_iota(jnp.int32, (c, c), 1)
    eye = (row == col).astype(F32)
    same = (row // INV_BASE) == (col // INV_BASE)

    def left_products(acc, pw, exp, limit):
        while exp < limit:
            if 2 * exp < limit:
                rhs = [jnp.concatenate([p, a.astype(BF16)], axis=1) for p, a in zip(pw, acc)]
                out = [jnp.dot(p, r, preferred_element_type=F32) for p, r in zip(pw, rhs)]
                pw = [o[:, :c].astype(BF16) for o in out]
                acc = [a + o[:, c:] for a, o in zip(acc, out)]
            else:
                acc = [a + jnp.dot(p, a.astype(BF16), preferred_element_type=F32) for p, a in zip(pw, acc)]
            exp *= 2
        return acc

    nd = [(-jnp.where(same, low, 0.0)) for low in lows]
    ndb = [x.astype(BF16) for x in nd]
    xs = [eye + x for x in nd]
    if INV_BASE > 2:
        sq = [jnp.dot(x, x, preferred_element_type=F32).astype(BF16) for x in ndb]
        xs = left_products(xs, sq, 2, INV_BASE)
    n_blocks = c // INV_BASE
    if n_blocks == 1:
        return xs
    xbs = [x.astype(BF16) for x in xs]
    ms = [jnp.dot(xb, jnp.where(same, 0.0, low).astype(BF16), preferred_element_type=F32).astype(BF16)
          for xb, low in zip(xbs, lows)]
    if n_blocks == 2:
        return [x - jnp.dot(m, xb, preferred_element_type=F32) for x, m, xb in zip(xs, ms, xbs)]
    out = [jnp.dot(m, jnp.concatenate([xb, m], axis=1), preferred_element_type=F32)
           for m, xb in zip(ms, xbs)]
    zs = [x - o[:, :c] for x, o in zip(xs, out)]
    m2 = [o[:, c:].astype(BF16) for o in out]
    return left_products(zs, m2, 2, n_blocks)


def _layer_prompt_kernel(x_ref, h0_ref, rc0_ref, s0_ref, gc0_ref,
                         wm_ref, bm_ref, wab_ref, bab_ref,
                         cw_ref, cb_ref, wg_ref, br_ref, bi_ref, lam_ref,
                         alog_ref, dtb_ref, nw_ref, wo_ref, g2_ref, b2_ref,
                         w1_ref, w3_ref, w2_ref, g3_ref, b3_ref,
                         out_ref, h_ref, rc_ref, s_ref, gc_ref,
                         cbuf, ybuf, h_sc, s_sc, *, tt, chunk, nt, n_tiles, alpha, heads_per_pass):
    step = pl.program_id(0)
    t = step % nt
    nc = tt // chunk

    @pl.when(step == 0)
    def _():
        ybuf[...] = jnp.zeros((tt, D_MODEL), F32)

    @pl.when(t == 0)
    def _():
        cbuf[0:TAIL, :] = jnp.zeros((TAIL, CONV_CH), F32)
        for j in range(N_GROUPS):
            lo = j * LANES
            cbuf[HIST:TAIL, j * CONV_GW + P_RGX:j * CONV_GW + P_RGX + LANES] = rc0_ref[0, :, lo:lo + LANES]
            for c, p in enumerate((P_Q, P_K, P_V)):
                cbuf[HIST:TAIL, j * CONV_GW + p:j * CONV_GW + p + LANES] = (
                    gc0_ref[0, :, c * GDN_QK + lo:c * GDN_QK + lo + LANES])
        h_sc[...] = h0_ref[0]
        s_sc[...] = s0_ref[0]

    y_prev = ybuf[...]
    ffn = _swiglu_steps(y_prev, w1_ref, w3_ref, w2_ref, ff_chunk=FFN_STEP)
    ffn_acc = [None]

    def ffn_step():
        nxt = next(ffn, None)
        if nxt is not None:
            ffn_acc[0] = nxt

    x = x_ref[0]
    xb = x.astype(BF16)
    ab = jnp.dot(xb, wab_ref[...], preferred_element_type=F32) + bab_ref[...]
    row = lax.broadcasted_iota(jnp.int32, (tt, LANES), 0)
    g, beta = _gdn_gates(ab, alog_ref[...], dtb_ref[...])
    rin = row % chunk
    d = 1
    while d < chunk:
        g = g + jnp.where(rin >= d, pltpu.roll(g, d, axis=0), 0.0)
        d *= 2
    g_t = g.T
    eg = jnp.exp(g)

    rw = lax.broadcasted_iota(jnp.int32, (chunk, chunk), 0)
    cl = lax.broadcasted_iota(jnp.int32, (chunk, chunk), 1)
    causal = rw >= cl
    strict = rw > cl
    sl = lambda arr, n: arr[n * chunk:(n + 1) * chunk]
    merged = []
    for j0 in range(0, N_GROUPS, heads_per_pass):
        heads = range(j0, j0 + heads_per_pass)

        gated_a, zgs, qs, ks, vs = {}, {}, {}, {}, {}
        for j in heads:
            lo = j * LANES
            cg = j * CONV_GW
            res = (jnp.dot(xb, wm_ref[:, j * GROUP_W:(j + 1) * GROUP_W], preferred_element_type=F32)
                   + bm_ref[:, j * GROUP_W:(j + 1) * GROUP_W])
            cbuf[TAIL:TAIL + tt, cg:cg + CONV_GW] = res[:, 0:CONV_GW]

            def conv(p):
                acc = cw_ref[CONV_W - 1:CONV_W, cg + p:cg + p + LANES] * res[:, p:p + LANES]
                for i in range(CONV_W - 1):
                    acc = acc + (cw_ref[i:i + 1, cg + p:cg + p + LANES]
                                 * cbuf[HIST + i:HIST + i + tt, cg + p:cg + p + LANES])
                return acc

            xc = conv(P_RGX) + cb_ref[:, lo:lo + LANES]
            c_lam = -RG_C * _softplus(-lam_ref[:, lo:lo + LANES])
            a, bt = _rg_gates(xc, wg_ref[j], br_ref[:, lo:lo + LANES], bi_ref[:, lo:lo + LANES], c_lam)
            hseq, h_last = _rg_scan(a, bt, h_sc[:, lo:lo + LANES])
            h_sc[:, lo:lo + LANES] = h_last
            gated_a[j] = (jax.nn.sigmoid(res[:, P_GA:P_GA + LANES])
                          * (hseq * jax.nn.gelu(res[:, P_RGY:P_RGY + LANES])))
            zgs[j] = jax.nn.sigmoid(res[:, P_GB:P_GB + LANES]) * jax.nn.silu(res[:, P_Z:P_Z + LANES])
            qs[j] = _l2norm(jax.nn.silu(conv(P_Q))) * (GDN_DK ** -0.5)
            ks[j] = _l2norm(jax.nn.silu(conv(P_K)))
            vs[j] = jax.nn.silu(conv(P_V))
            ffn_step()

        c0, c1 = j0 * CONV_GW, (j0 + heads_per_pass) * CONV_GW
        cbuf[0:TAIL, c0:c1] = cbuf[tt:tt + TAIL, c0:c1]

        probs = [(j, n) for j in heads for n in range(nc)]
        gcol = {p: sl(g, p[1])[:, p[0]:p[0] + 1] for p in probs}
        egc = {p: sl(eg, p[1])[:, p[0]:p[0] + 1] for p in probs}
        bcol = {p: sl(beta, p[1])[:, GDN_HEADS + p[0]:GDN_HEADS + p[0] + 1] for p in probs}
        decay = {p: jnp.exp(jnp.where(causal, gcol[p] - g_t[p[0]:p[0] + 1, p[1] * chunk:(p[1] + 1) * chunk],
                                      0.0)) for p in probs}
        kn = {p: sl(ks[p[0]], p[1]) for p in probs}
        qn = {p: sl(qs[p[0]], p[1]) for p in probs}
        kb = {p: kn[p] * bcol[p] for p in probs}
        knb = {p: kn[p].astype(BF16) for p in probs}
        kk = {p: lax.dot_general(kb[p].astype(BF16), knb[p], (((1,), (1,)), ((), ())),
                                 preferred_element_type=F32) for p in probs}
        qk = {p: lax.dot_general(qn[p].astype(BF16), knb[p], (((1,), (1,)), ((), ())),
                                 preferred_element_type=F32) for p in probs}
        lows = [jnp.where(strict, kk[p] * decay[p], 0.0) for p in probs]
        a_intra = {p: jnp.where(causal, qk[p] * decay[p], 0.0).astype(BF16) for p in probs}
        tinv = dict(zip(probs, _tri_inverse_many(lows, chunk)))
        sol = {p: jnp.dot(tinv[p].astype(BF16),
                          jnp.concatenate([sl(vs[p[0]], p[1]) * bcol[p], kb[p] * egc[p]], axis=1).astype(BF16),
                          preferred_element_type=F32) for p in probs}
        g_last = {p: gcol[p][chunk - 1:chunk, :] for p in probs}
        wq = {p: jnp.concatenate([sol[p][:, GDN_DV:], qn[p] * egc[p]], axis=0).astype(BF16) for p in probs}
        k_tail = {p: (kn[p] * jnp.exp(g_last[p] - gcol[p])).astype(BF16) for p in probs}

        states = {j: s_sc[j] for j in heads}
        outs = {}
        for n in range(nc):
            ws = {j: jnp.dot(wq[(j, n)], states[j].astype(BF16), preferred_element_type=F32) for j in heads}
            v_new = {j: (sol[(j, n)][:, :GDN_DV] - ws[j][:chunk]).astype(BF16) for j in heads}
            for j in heads:
                outs[(j, n)] = ws[j][chunk:] + jnp.dot(a_intra[(j, n)], v_new[j], preferred_element_type=F32)
            states = {j: states[j] * jnp.exp(g_last[(j, n)])
                      + lax.dot_general(k_tail[(j, n)], v_new[j], (((0,), (0,)), ((), ())),
                                        preferred_element_type=F32) for j in heads}
            if n % 2 == 0:
                ffn_step()
        for j in heads:
            s_sc[j] = states[j]

        for j in heads:
            o = outs[(j, 0)] if nc == 1 else jnp.concatenate([outs[(j, n)] for n in range(nc)], axis=0)
            merged.append((gated_a[j] + _rms_gate(o, nw_ref[...], zgs[j])).astype(BF16))

    for _ in ffn:
        ffn_acc[0] = _
    out_ref[0] = _layer_norm(alpha * y_prev + 0.5 * ffn_acc[0], g3_ref[...], b3_ref[...])

    m = jnp.concatenate(merged, axis=1)
    y = jnp.dot(m, wo_ref[...], preferred_element_type=F32)
    ybuf[...] = _layer_norm(alpha * x + y, g2_ref[...], b2_ref[...])

    @pl.when(jnp.logical_and(step < n_tiles, t == nt - 1))
    def _():
        h_ref[0] = h_sc[...]
        for j in range(N_GROUPS):
            lo = j * LANES
            rc_ref[0, :, lo:lo + LANES] = cbuf[HIST:TAIL, j * CONV_GW + P_RGX:j * CONV_GW + P_RGX + LANES]
            for c, p in enumerate((P_Q, P_K, P_V)):
                gc_ref[0, :, c * GDN_QK + lo:c * GDN_QK + lo + LANES] = (
                    cbuf[HIST:TAIL, j * CONV_GW + p:j * CONV_GW + p + LANES])
        s_ref[0] = s_sc[...]


def _layer_prompt(x, h0, rc0, s0, gc0, lp, *, alpha, tt, chunk):
    bsz, seq, _ = x.shape
    nt = seq // tt
    n_tiles = bsz * nt
    last = n_tiles - 1
    cur = lambda s: jnp.minimum(s, last)
    per_b3 = lambda s: (cur(s) // nt, 0, 0)
    per_b4 = lambda s: (cur(s) // nt, 0, 0, 0)
    kern = functools.partial(_layer_prompt_kernel, tt=tt, chunk=chunk, nt=nt, n_tiles=n_tiles,
                             alpha=alpha, heads_per_pass=N_GROUPS // 2)
    out, h, rc, s, gc = pl.pallas_call(
        kern,
        grid=(n_tiles + 1,),
        in_specs=[pl.BlockSpec((1, tt, D_MODEL), lambda s: (cur(s), 0, 0)),
                  pl.BlockSpec((1, 1, D_RNN), per_b3),
                  pl.BlockSpec((1, CONV_W - 1, D_RNN), per_b3),
                  pl.BlockSpec((1, GDN_HEADS, GDN_DK, GDN_DV), per_b4),
                  pl.BlockSpec((1, CONV_W - 1, 3 * GDN_QK), per_b3),
                  _const_spec((D_MODEL, D_MAIN)), _const_spec((1, D_MAIN)),
                  _const_spec((D_MODEL, LANES)), _const_spec((1, LANES)),
                  _const_spec((CONV_W, CONV_CH)), _const_spec((1, D_RNN)),
                  _const_spec((N_GROUPS, LANES, 2 * LANES)),
                  _const_spec((1, D_RNN)), _const_spec((1, D_RNN)), _const_spec((1, D_RNN)),
                  _const_spec((1, LANES)), _const_spec((1, LANES)), _const_spec((1, GDN_DV)),
                  _const_spec((D_MODEL, D_MODEL)), _const_spec((1, D_MODEL)), _const_spec((1, D_MODEL)),
                  _const_spec((D_MODEL, D_FF)), _const_spec((D_MODEL, D_FF)),
                  _const_spec((D_FF, D_MODEL)), _const_spec((1, D_MODEL)), _const_spec((1, D_MODEL))],
        out_specs=[pl.BlockSpec((1, tt, D_MODEL), lambda s: (jnp.maximum(s - 1, 0), 0, 0)),
                   pl.BlockSpec((1, 1, D_RNN), per_b3),
                   pl.BlockSpec((1, CONV_W - 1, D_RNN), per_b3),
                   pl.BlockSpec((1, GDN_HEADS, GDN_DK, GDN_DV), per_b4),
                   pl.BlockSpec((1, CONV_W - 1, 3 * GDN_QK), per_b3)],
        out_shape=[jax.ShapeDtypeStruct((n_tiles, tt, D_MODEL), F32),
                   jax.ShapeDtypeStruct((bsz, 1, D_RNN), F32),
                   jax.ShapeDtypeStruct((bsz, CONV_W - 1, D_RNN), F32),
                   jax.ShapeDtypeStruct((bsz, GDN_HEADS, GDN_DK, GDN_DV), F32),
                   jax.ShapeDtypeStruct((bsz, CONV_W - 1, 3 * GDN_QK), F32)],
        scratch_shapes=[pltpu.VMEM((tt + TAIL, CONV_CH), F32),
                        pltpu.VMEM((tt, D_MODEL), F32),
                        pltpu.VMEM((1, D_RNN), F32),
                        pltpu.VMEM((GDN_HEADS, GDN_DK, GDN_DV), F32)],
        compiler_params=pltpu.CompilerParams(dimension_semantics=("arbitrary",),
                                             vmem_limit_bytes=VMEM_LIMIT),
        name="layer_prompt",
    )(x.reshape(n_tiles, tt, D_MODEL), h0, rc0, s0, gc0,
      lp["w_main"], lp["b_main"], lp["w_ab"], lp["b_ab"],
      lp["cw"], lp["cb"], lp["wg"], lp["br"], lp["bi"], lp["lam"],
      lp["alog"], lp["dtb"], lp["nw"], lp["w_o"], lp["ln2_g"], lp["ln2_b"],
      *lp["f2"], lp["ln3_g"], lp["ln3_b"])
    return out.reshape(bsz, seq, D_MODEL), h, rc, s, gc


def _mixer_sample_kernel(main_ref, ab_ref, h0_ref, rc0_ref, s0_ref, gc0_ref,
                         cw_ref, cb_ref, wg_ref, br_ref, bi_ref, lam_ref,
                         alog_ref, dtb_ref, nw_ref, *refs, bb, has_acc):
    merged_ref, h_ref, rc_ref, s_ref, gc_ref = refs[1:] if has_acc else refs
    nprev = CONV_W - 1

    def conv(hist_ref, ch, off_hist, j, p):
        cc = j * CONV_GW + p
        acc = cw_ref[nprev:CONV_W, cc:cc + LANES] * main_ref[:, j * GROUP_W + p:j * GROUP_W + p + LANES]
        for i in range(nprev):
            acc = acc + (cw_ref[i:i + 1, cc:cc + LANES]
                         * hist_ref[:, i * ch + off_hist:i * ch + off_hist + LANES])
        return acc

    def part(j, p):
        return main_ref[:, j * GROUP_W + p:j * GROUP_W + p + LANES]

    qkv_w = 3 * GDN_QK
    rc_ref[:, 0:2 * D_RNN] = rc0_ref[:, D_RNN:3 * D_RNN]
    gc_ref[:, 0:2 * qkv_w] = gc0_ref[:, qkv_w:3 * qkv_w]

    g, beta = _gdn_gates(ab_ref[...], alog_ref[...], dtb_ref[...])
    eg = jnp.exp(g)

    for j in range(N_GROUPS):
        lo = j * LANES
        rc_ref[:, 2 * D_RNN + lo:2 * D_RNN + lo + LANES] = part(j, P_RGX)
        for c, p in enumerate((P_Q, P_K, P_V)):
            gc_ref[:, 2 * qkv_w + c * GDN_QK + lo:2 * qkv_w + c * GDN_QK + lo + LANES] = part(j, p)

        xc = conv(rc0_ref, D_RNN, lo, j, P_RGX) + cb_ref[:, lo:lo + LANES]
        c_lam = -RG_C * _softplus(-lam_ref[:, lo:lo + LANES])
        a, bt = _rg_gates(xc, wg_ref[j], br_ref[:, lo:lo + LANES], bi_ref[:, lo:lo + LANES], c_lam)
        hseq = a * h0_ref[:, lo:lo + LANES] + bt
        h_ref[:, lo:lo + LANES] = hseq

        qn = _l2norm(jax.nn.silu(conv(gc0_ref, qkv_w, lo, j, P_Q))) * (GDN_DK ** -0.5)
        kn = _l2norm(jax.nn.silu(conv(gc0_ref, qkv_w, GDN_QK + lo, j, P_K)))
        v = jax.nn.silu(conv(gc0_ref, qkv_w, 2 * GDN_QK + lo, j, P_V))
        qn_t = qn.T
        kn_t = kn.T
        o_rows = []
        for b in range(bb):
            s = s0_ref[b, j]
            kcol = kn_t[:, b:b + 1]
            egb = eg[b:b + 1, j:j + 1]
            ks = jnp.sum(s * kcol, axis=0, keepdims=True)
            v_new = beta[b:b + 1, GDN_HEADS + j:GDN_HEADS + j + 1] * (v[b:b + 1, :] - egb * ks)
            s_new = s * egb + kcol * v_new
            s_ref[b, j] = s_new
            o_rows.append(jnp.sum(s_new * qn_t[:, b:b + 1], axis=0, keepdims=True))
        o = jnp.concatenate(o_rows, axis=0)
        gated_a = jax.nn.sigmoid(part(j, P_GA)) * (hseq * jax.nn.gelu(part(j, P_RGY)))
        zg = jax.nn.sigmoid(part(j, P_GB)) * jax.nn.silu(part(j, P_Z))
        merged_ref[:, lo:lo + LANES] = (gated_a + _rms_gate(o, nw_ref[...], zg)).astype(BF16)


def _mixer_sample(main, ab, h0, rc0, s_all, gc0, lp, *, layer, s_acc, bb):
    bsz = main.shape[0]
    row2 = lambda i: (i, 0)
    s_spec = pl.BlockSpec((None, bb, GDN_HEADS, GDN_DK, GDN_DV), lambda i: (layer, i, 0, 0, 0))
    has_acc = s_acc is not None
    kern = functools.partial(_mixer_sample_kernel, bb=bb, has_acc=has_acc)
    qkv3 = 3 * 3 * GDN_QK
    in_specs = [pl.BlockSpec((bb, D_MAIN), row2),
                pl.BlockSpec((bb, LANES), row2),
                pl.BlockSpec((bb, D_RNN), row2),
                pl.BlockSpec((bb, 3 * D_RNN), row2),
                s_spec,
                pl.BlockSpec((bb, qkv3), row2),
                _const_spec((CONV_W, CONV_CH)), _const_spec((1, D_RNN)),
                _const_spec((N_GROUPS, LANES, 2 * LANES)),
                _const_spec((1, D_RNN)), _const_spec((1, D_RNN)), _const_spec((1, D_RNN)),
                _const_spec((1, LANES)), _const_spec((1, LANES)), _const_spec((1, GDN_DV))]
    args = [main, ab, h0, rc0, s_all, gc0, lp["cw"], lp["cb"], lp["wg"], lp["br"], lp["bi"], lp["lam"],
            lp["alog"], lp["dtb"], lp["nw"]]
    aliases = {}
    if has_acc:
        in_specs.append(pl.BlockSpec(memory_space=pl.ANY))
        args.append(s_acc)
        aliases = {len(args) - 1: 3}
    return pl.pallas_call(
        kern,
        grid=(bsz // bb,),
        in_specs=in_specs,
        out_specs=[pl.BlockSpec((bb, D_MODEL), row2),
                   pl.BlockSpec((bb, D_RNN), row2),
                   pl.BlockSpec((bb, 3 * D_RNN), row2),
                   s_spec,
                   pl.BlockSpec((bb, qkv3), row2)],
        out_shape=[jax.ShapeDtypeStruct((bsz, D_MODEL), BF16),
                   jax.ShapeDtypeStruct((bsz, D_RNN), F32),
                   jax.ShapeDtypeStruct((bsz, 3 * D_RNN), F32),
                   jax.ShapeDtypeStruct(s_all.shape, F32),
                   jax.ShapeDtypeStruct((bsz, qkv3), F32)],
        input_output_aliases=aliases,
        compiler_params=pltpu.CompilerParams(dimension_semantics=("arbitrary",),
                                             vmem_limit_bytes=VMEM_LIMIT),
        name="mixer_sample",
    )(*args)


def _prep_layer(p):
    (ln1_g, ln1_b, ffn1_w1, ffn1_w3, ffn1_w2, w_in, b_in, rg_conv_w, rg_conv_b, rg_wr, rg_br,
     rg_wi, rg_bi, rg_lambda, gdn_conv_w, gdn_a_log, gdn_dt_bias, gdn_norm_w, w_o, ln2_g, ln2_b,
     ffn2_w1, ffn2_w3, ffn2_w2, ln3_g, ln3_b) = p
    row = lambda v: v.reshape(1, -1)

    def reorder(w):
        a0 = 2 * D_RNN + 2 * GDN_QK + 2 * GDN_VW
        g0 = a0 + 2 * GDN_HEADS
        starts = (0, 2 * D_RNN, 3 * D_RNN, 4 * D_RNN, D_RNN, 5 * D_RNN, g0, g0 + D_MODEL)
        pieces = [w[..., s:s + D_RNN].reshape(w.shape[:-1] + (N_GROUPS, 1, LANES)) for s in starts]
        main = jnp.concatenate(pieces, axis=-2).reshape(w.shape[:-1] + (D_MAIN,))
        ab = w[..., a0:g0]
        ab = jnp.pad(ab, [(0, 0)] * (w.ndim - 1) + [(0, LANES - 2 * GDN_HEADS)])
        return main, ab

    w_main, w_ab = reorder(w_in)
    b_main, b_ab = reorder(row(b_in))

    def pair_blocks(w):
        w = w.reshape(N_GROUPS, 2, RG_BLOCK, RG_BLOCK)
        z = jnp.zeros((N_GROUPS, RG_BLOCK, RG_BLOCK), w.dtype)
        top = jnp.concatenate([w[:, 0], z], axis=2)
        bot = jnp.concatenate([z, w[:, 1]], axis=2)
        return jnp.concatenate([top, bot], axis=1)

    wg = jnp.concatenate([pair_blocks(rg_wr), pair_blocks(rg_wi)], axis=2).astype(BF16)
    cw = jnp.concatenate([rg_conv_w.reshape(CONV_W, N_GROUPS, 1, LANES),
                          gdn_conv_w.reshape(CONV_W, 3, N_GROUPS, LANES).transpose(0, 2, 1, 3)],
                         axis=2).reshape(CONV_W, CONV_CH)
    pad_h = lambda v: jnp.pad(row(v), [(0, 0), (0, LANES - GDN_HEADS)])
    return dict(
        ln1_g=row(ln1_g), ln1_b=row(ln1_b), ln2_g=row(ln2_g), ln2_b=row(ln2_b),
        ln3_g=row(ln3_g), ln3_b=row(ln3_b),
        f1=(ffn1_w1.astype(BF16), ffn1_w3.astype(BF16), ffn1_w2.astype(BF16)),
        f2=(ffn2_w1.astype(BF16), ffn2_w3.astype(BF16), ffn2_w2.astype(BF16)),
        w_main=w_main.astype(BF16), b_main=b_main, w_ab=w_ab.astype(BF16), b_ab=b_ab,
        w_o=w_o.astype(BF16),
        cw=cw, cb=row(rg_conv_b),
        wg=wg, br=row(rg_br), bi=row(rg_bi), lam=row(rg_lambda),
        alog=pad_h(gdn_a_log), dtb=pad_h(gdn_dt_bias), nw=row(gdn_norm_w))


def _trunk_prompt(x, layers, *, alpha, tm, tt, chunk):
    bsz, seq, _ = x.shape
    h0 = jnp.zeros((bsz, 1, D_RNN), x.dtype)
    rc0 = jnp.zeros((bsz, CONV_W - 1, D_RNN), x.dtype)
    s0 = jnp.zeros((bsz, GDN_HEADS, GDN_DK, GDN_DV), x.dtype)
    gc0 = jnp.zeros((bsz, CONV_W - 1, 3 * GDN_QK), x.dtype)
    hs, rcs, ss, gcs = [], [], [], []
    for lp in layers:
        x1 = _ffn_ln(x.reshape(bsz * seq, D_MODEL), *lp["f1"], lp["ln1_g"], lp["ln1_b"],
                     alpha=alpha, tm=tm)
        x, h, rc, s, gc = _layer_prompt(x1.reshape(bsz, seq, D_MODEL), h0, rc0, s0, gc0, lp,
                                        alpha=alpha, tt=min(tt, seq), chunk=chunk)
        hs.append(h.reshape(bsz, D_RNN))
        rcs.append(rc)
        ss.append(s)
        gcs.append(gc)
    return x, jnp.stack(hs), jnp.stack(rcs), jnp.stack(ss), jnp.stack(gcs)


def _trunk_sample(x, h0, rc0, s0, gc0, layers, *, alpha, bb):
    bsz = x.shape[0]
    xf = x.reshape(bsz, D_MODEL)
    hs, rcs, gcs = [], [], []
    s_acc = None
    for l, lp in enumerate(layers):
        xf = _ffn_ln(xf, *lp["f1"], lp["ln1_g"], lp["ln1_b"], alpha=alpha, tm=bsz)
        main, ab = _in_proj(xf, lp["w_main"], lp["b_main"], lp["w_ab"], lp["b_ab"], tm=bsz)
        merged, h, rc, s_acc, gc = _mixer_sample(
            main, ab, h0[l], rc0[l].reshape(bsz, -1), s0, gc0[l].reshape(bsz, -1), lp,
            layer=l, s_acc=s_acc, bb=bb)
        xf = _out_proj_ln(xf, merged, lp["w_o"], lp["ln2_g"], lp["ln2_b"], alpha=alpha, tm=bsz)
        xf = _ffn_ln(xf, *lp["f2"], lp["ln3_g"], lp["ln3_b"], alpha=alpha, tm=bsz)
        hs.append(h)
        rcs.append(rc.reshape(bsz, CONV_W - 1, D_RNN))
        gcs.append(gc.reshape(bsz, CONV_W - 1, 3 * GDN_QK))
    return xf.reshape(bsz, 1, D_MODEL), jnp.stack(hs), jnp.stack(rcs), s_acc, jnp.stack(gcs)


def kernel(x_prompt, x_sample, state_rglru_h, state_rglru_conv, state_gdn_S, state_gdn_conv, ln1_g, ln1_b, ffn1_w1, ffn1_w3, ffn1_w2, w_in, b_in, rg_conv_w, rg_conv_b, rg_wr, rg_br, rg_wi, rg_bi, rg_lambda, gdn_conv_w, gdn_a_log, gdn_dt_bias, gdn_norm_w, w_o, ln2_g, ln2_b, ffn2_w1, ffn2_w3, ffn2_w2, ln3_g, ln3_b):
    params = (ln1_g, ln1_b, ffn1_w1, ffn1_w3, ffn1_w2, w_in, b_in, rg_conv_w, rg_conv_b, rg_wr, rg_br,
              rg_wi, rg_bi, rg_lambda, gdn_conv_w, gdn_a_log, gdn_dt_bias, gdn_norm_w, w_o, ln2_g, ln2_b,
              ffn2_w1, ffn2_w3, ffn2_w2, ln3_g, ln3_b)
    layers = [_prep_layer([p[l] for p in params]) for l in range(ln1_g.shape[0])]
    y_p, p_h, p_rc, p_s, p_gc = _trunk_prompt(x_prompt, layers, alpha=ALPHA, tm=512, tt=256, chunk=64)
    y_s, s_h, s_rc, s_s, s_gc = _trunk_sample(
        x_sample, state_rglru_h, state_rglru_conv, state_gdn_S, state_gdn_conv, layers,
        alpha=ALPHA, bb=8)
    return (y_p, y_s, p_h, p_rc, p_s, p_gc, s_h, s_rc, s_s, s_gc)
```

```python
import functools

import jax
import jax.numpy as jnp
from jax import lax
from jax.experimental import pallas as pl
from jax.experimental.pallas import tpu as pltpu

F32 = jnp.float32
BF16 = jnp.bfloat16

D_MODEL = 1024
DEPTH = 4
ALPHA = (2 * DEPTH) ** 0.25
D_RNN = D_MODEL
RG_BLOCK = 64
RG_C = 8.0
CONV_W = 4
GDN_HEADS = 8
GDN_DK = 128
GDN_DV = 128
GDN_QK = GDN_HEADS * GDN_DK
GDN_VW = GDN_HEADS * GDN_DV
D_FF = 2816
LN_EPS = 1e-5
RMS_EPS = 1e-6
LANES = 128
SUBLANES = 8
N_GROUPS = D_RNN // LANES

P_RGX, P_Q, P_K, P_V, P_RGY, P_Z, P_GA, P_GB = (i * LANES for i in range(8))
GROUP_W = 8 * LANES
CONV_GW = 4 * LANES
CONV_CH = N_GROUPS * CONV_GW
D_MAIN = N_GROUPS * GROUP_W
TAIL = SUBLANES
HIST = TAIL - (CONV_W - 1)
INV_BASE = 16

VMEM_LIMIT = 56 * 1024 * 1024


def _mm(a, b):
    return jnp.dot(a.astype(BF16), b.astype(BF16), preferred_element_type=F32)


def _softplus(x):
    return jnp.maximum(x, 0.0) + jnp.log1p(jnp.exp(-jnp.abs(x)))


def _layer_norm(r, g, b):
    mu = jnp.mean(r, axis=-1, keepdims=True)
    c = r - mu
    var = jnp.mean(c * c, axis=-1, keepdims=True)
    return c * lax.rsqrt(var + LN_EPS) * g + b


def _ffn_ln_math(x, w1_ref, w3_ref, w2_ref, g, b, *, alpha, ff_chunk):
    xb = x.astype(BF16)
    acc = None
    for c in range(D_FF // ff_chunk):
        sl = slice(c * ff_chunk, (c + 1) * ff_chunk)
        h1 = jnp.dot(xb, w1_ref[:, sl], preferred_element_type=F32)
        h3 = jnp.dot(xb, w3_ref[:, sl], preferred_element_type=F32)
        hh = (jax.nn.silu(h1) * h3).astype(BF16)
        y = jnp.dot(hh, w2_ref[sl, :], preferred_element_type=F32)
        acc = y if acc is None else acc + y
    return _layer_norm(alpha * x + 0.5 * acc, g, b)


def _ffn_ln_kernel(x_ref, w1_ref, w3_ref, w2_ref, g_ref, b_ref, o_ref, *, alpha, ff_chunk):
    o_ref[...] = _ffn_ln_math(x_ref[...], w1_ref, w3_ref, w2_ref, g_ref[...], b_ref[...],
                              alpha=alpha, ff_chunk=ff_chunk)


def _const_spec(shape):
    nd = len(shape)
    return pl.BlockSpec(shape, lambda *_: (0,) * nd, pipeline_mode=pl.Buffered(1))


def _ffn_ln(x, w1, w3, w2, g, b, *, alpha, tm):
    m = x.shape[0]
    return pl.pallas_call(
        functools.partial(_ffn_ln_kernel, alpha=alpha, ff_chunk=D_FF // 2),
        grid=(m // tm,),
        in_specs=[pl.BlockSpec((tm, D_MODEL), lambda i: (i, 0)),
                  _const_spec((D_MODEL, D_FF)), _const_spec((D_MODEL, D_FF)),
                  _const_spec((D_FF, D_MODEL)),
                  _const_spec((1, D_MODEL)), _const_spec((1, D_MODEL))],
        out_specs=pl.BlockSpec((tm, D_MODEL), lambda i: (i, 0)),
        out_shape=jax.ShapeDtypeStruct((m, D_MODEL), F32),
        compiler_params=pltpu.CompilerParams(dimension_semantics=("arbitrary",),
                                             vmem_limit_bytes=VMEM_LIMIT),
        name="ffn_ln",
    )(x, w1, w3, w2, g, b)


def _in_proj_kernel(x_ref, w_ref, b_ref, wab_ref, bab_ref, main_ref, ab_ref, *, n_chunk):
    xb = x_ref[...].astype(BF16)
    for c in range(D_MAIN // n_chunk):
        sl = slice(c * n_chunk, (c + 1) * n_chunk)
        main_ref[:, sl] = jnp.dot(xb, w_ref[:, sl], preferred_element_type=F32) + b_ref[:, sl]
    ab_ref[...] = jnp.dot(xb, wab_ref[...], preferred_element_type=F32) + bab_ref[...]


def _in_proj(x, w_main, b_main, w_ab, b_ab, *, tm):
    m = x.shape[0]
    return pl.pallas_call(
        functools.partial(_in_proj_kernel, n_chunk=GROUP_W),
        grid=(m // tm,),
        in_specs=[pl.BlockSpec((tm, D_MODEL), lambda i: (i, 0)),
                  _const_spec((D_MODEL, D_MAIN)), _const_spec((1, D_MAIN)),
                  _const_spec((D_MODEL, LANES)), _const_spec((1, LANES))],
        out_specs=[pl.BlockSpec((tm, D_MAIN), lambda i: (i, 0)),
                   pl.BlockSpec((tm, LANES), lambda i: (i, 0))],
        out_shape=[jax.ShapeDtypeStruct((m, D_MAIN), F32),
                   jax.ShapeDtypeStruct((m, LANES), F32)],
        compiler_params=pltpu.CompilerParams(dimension_semantics=("arbitrary",),
                                             vmem_limit_bytes=VMEM_LIMIT),
        name="in_proj",
    )(x, w_main, b_main, w_ab, b_ab)


def _out_proj_ln_kernel(x_ref, m_ref, w_ref, g_ref, b_ref, o_ref, *, alpha):
    y = jnp.dot(m_ref[...], w_ref[...], preferred_element_type=F32)
    o_ref[...] = _layer_norm(alpha * x_ref[...] + y, g_ref[...], b_ref[...])


def _out_proj_ln(x, merged, w_o, g, b, *, alpha, tm):
    m = x.shape[0]
    return pl.pallas_call(
        functools.partial(_out_proj_ln_kernel, alpha=alpha),
        grid=(m // tm,),
        in_specs=[pl.BlockSpec((tm, D_MODEL), lambda i: (i, 0)),
                  pl.BlockSpec((tm, D_MODEL), lambda i: (i, 0)),
                  _const_spec((D_MODEL, D_MODEL)),
                  _const_spec((1, D_MODEL)), _const_spec((1, D_MODEL))],
        out_specs=pl.BlockSpec((tm, D_MODEL), lambda i: (i, 0)),
        out_shape=jax.ShapeDtypeStruct((m, D_MODEL), F32),
        compiler_params=pltpu.CompilerParams(dimension_semantics=("arbitrary",),
                                             vmem_limit_bytes=VMEM_LIMIT),
        name="out_proj_ln",
    )(x, merged, w_o, g, b)


def _rg_gates(xc, wg, br, bi, c_lam):
    pre = _mm(xc, wg)
    r = jax.nn.sigmoid(pre[:, :LANES] + br)
    i = jax.nn.sigmoid(pre[:, LANES:] + bi)
    log_a = c_lam * r
    a = jnp.exp(log_a)
    bt = jnp.sqrt(1.0 - a * a) * (i * xc)
    return a, bt


def _l2norm(t):
    return t * lax.rsqrt(jnp.sum(t * t, axis=-1, keepdims=True) + RMS_EPS)


def _rms_gate(o, nw, zg):
    return o * lax.rsqrt(jnp.mean(o * o, axis=-1, keepdims=True) + RMS_EPS) * nw * zg


def _gdn_gates(ab, alog, dtb):
    g = -jnp.exp(alog) * _softplus(ab + dtb)
    beta = jax.nn.sigmoid(ab)
    return g, beta


def _rg_scan(a, b, h_prev):
    n = a.shape[0]
    nv = n // SUBLANES
    a3 = a.reshape(nv, SUBLANES, LANES)
    b3 = b.reshape(nv, SUBLANES, LANES)
    sub = lax.broadcasted_iota(jnp.int32, (nv, SUBLANES, LANES), 1)
    d = 1
    while d < SUBLANES:
        keep = sub >= d
        a_sh = jnp.where(keep, pltpu.roll(a3, d, axis=1), 1.0)
        b_sh = jnp.where(keep, pltpu.roll(b3, d, axis=1), 0.0)
        b3 = a3 * b_sh + b3
        a3 = a3 * a_sh
        d *= 2
    carry = h_prev
    hs = []
    for v in range(nv):
        hv = a3[v] * carry + b3[v]
        hs.append(hv)
        carry = hv[SUBLANES - 1:SUBLANES, :]
    return jnp.concatenate(hs, axis=0), carry


def _tri_inverse_many(lows, c):
    row = lax.broadcasted_iota(jnp.int32, (c, c), 0)
    col = lax.broadcasted_iota(jnp.int32, (c, c), 1)
    eye = (row == col).astype(F32)
    same = (row // INV_BASE) == (col // INV_BASE)

    def left_products(acc, pw, exp, limit):
        while exp < limit:
            if 2 * exp < limit:
                rhs = [jnp.concatenate([p, a.astype(BF16)], axis=1) for p, a in zip(pw, acc)]
                out = [jnp.dot(p, r, preferred_element_type=F32) for p, r in zip(pw, rhs)]
                pw = [o[:, :c].astype(BF16) for o in out]
                acc = [a + o[:, c:] for a, o in zip(acc, out)]
            else:
                acc = [a + jnp.dot(p, a.astype(BF16), preferred_element_type=F32) for p, a in zip(pw, acc)]
            exp *= 2
        return acc

    nd = [(-jnp.where(same, low, 0.0)) for low in lows]
    ndb = [x.astype(BF16) for x in nd]
    xs = [eye + x for x in nd]
    if INV_BASE > 2:
        sq = [jnp.dot(x, x, preferred_element_type=F32).astype(BF16) for x in ndb]
        xs = left_products(xs, sq, 2, INV_BASE)
    n_blocks = c // INV_BASE
    if n_blocks == 1:
        return xs
    xbs = [x.astype(BF16) for x in xs]
    ms = [jnp.dot(xb, jnp.where(same, 0.0, low).astype(BF16), preferred_element_type=F32).astype(BF16)
          for xb, low in zip(xbs, lows)]
    if n_blocks == 2:
        return [x - jnp.dot(m, xb, preferred_element_type=F32) for x, m, xb in zip(xs, ms, xbs)]
    out = [jnp.dot(m, jnp.concatenate([xb, m], axis=1), preferred_element_type=F32)
           for m, xb in zip(ms, xbs)]
    zs = [x - o[:, :c] for x, o in zip(xs, out)]
    m2 = [o[:, c:].astype(BF16) for o in out]
    return left_products(zs, m2, 2, n_blocks)


def _layer_prompt_kernel(x_ref, h0_ref, rc0_ref, s0_ref, gc0_ref,
                         wm_ref, bm_ref, wab_ref, bab_ref,
                         cw_ref, cb_ref, wg_ref, br_ref, bi_ref, lam_ref,
                         alog_ref, dtb_ref, nw_ref, wo_ref, g2_ref, b2_ref,
                         out_ref, h_ref, rc_ref, s_ref, gc_ref,
                         cbuf, h_sc, s_sc, *, tt, chunk, alpha, heads_per_pass):
    t = pl.program_id(1)
    nt = pl.num_programs(1)
    nc = tt // chunk

    @pl.when(t == 0)
    def _():
        cbuf[0:TAIL, :] = jnp.zeros((TAIL, CONV_CH), F32)
        for j in range(N_GROUPS):
            lo = j * LANES
            cbuf[HIST:TAIL, j * CONV_GW + P_RGX:j * CONV_GW + P_RGX + LANES] = rc0_ref[0, :, lo:lo + LANES]
            for c, p in enumerate((P_Q, P_K, P_V)):
                cbuf[HIST:TAIL, j * CONV_GW + p:j * CONV_GW + p + LANES] = (
                    gc0_ref[0, :, c * GDN_QK + lo:c * GDN_QK + lo + LANES])
        h_sc[...] = h0_ref[0]
        s_sc[...] = s0_ref[0]

    x = x_ref[0]
    xb = x.astype(BF16)
    ab_t = (jnp.dot(xb, wab_ref[...], preferred_element_type=F32) + bab_ref[...]).T
    g_t = -jnp.exp(alog_ref[...]) * _softplus(ab_t[0:GDN_HEADS] + dtb_ref[...])
    beta_t = jax.nn.sigmoid(ab_t[GDN_HEADS:2 * GDN_HEADS])
    lin = lax.broadcasted_iota(jnp.int32, (GDN_HEADS, tt), 1) % chunk
    d = 1
    while d < chunk:
        g_t = g_t + jnp.where(lin >= d, pltpu.roll(g_t, d, axis=1), 0.0)
        d *= 2
    cols = jnp.concatenate([g_t, beta_t, jnp.exp(g_t),
                            jnp.zeros((LANES - 3 * GDN_HEADS, tt), F32)], axis=0).T
    L_G, L_BETA, L_EG = 0, GDN_HEADS, 2 * GDN_HEADS
    c_lam_all = -RG_C * _softplus(-lam_ref[...])

    rw = lax.broadcasted_iota(jnp.int32, (chunk, chunk), 0)
    cl = lax.broadcasted_iota(jnp.int32, (chunk, chunk), 1)
    causal = rw >= cl
    strict = rw > cl
    sl = lambda arr, n: arr[n * chunk:(n + 1) * chunk]
    merged = []
    for j0 in range(0, N_GROUPS, heads_per_pass):
        heads = range(j0, j0 + heads_per_pass)

        gated_a, zgs, qs, ks, vs = {}, {}, {}, {}, {}
        for j in heads:
            lo = j * LANES
            cg = j * CONV_GW
            res = (jnp.dot(xb, wm_ref[:, j * GROUP_W:(j + 1) * GROUP_W], preferred_element_type=F32)
                   + bm_ref[:, j * GROUP_W:(j + 1) * GROUP_W])
            cbuf[TAIL:TAIL + tt, cg:cg + CONV_GW] = res[:, 0:CONV_GW]

            def conv(p):
                acc = cw_ref[CONV_W - 1:CONV_W, cg + p:cg + p + LANES] * res[:, p:p + LANES]
                for i in range(CONV_W - 1):
                    acc = acc + (cw_ref[i:i + 1, cg + p:cg + p + LANES]
                                 * cbuf[HIST + i:HIST + i + tt, cg + p:cg + p + LANES])
                return acc

            xc = conv(P_RGX) + cb_ref[:, lo:lo + LANES]
            a, bt = _rg_gates(xc, wg_ref[j], br_ref[:, lo:lo + LANES], bi_ref[:, lo:lo + LANES],
                              c_lam_all[j:j + 1, :])
            hseq, h_last = _rg_scan(a, bt, h_sc[:, lo:lo + LANES])
            h_sc[:, lo:lo + LANES] = h_last
            gated_a[j] = (jax.nn.sigmoid(res[:, P_GA:P_GA + LANES])
                          * (hseq * jax.nn.gelu(res[:, P_RGY:P_RGY + LANES])))
            zgs[j] = jax.nn.sigmoid(res[:, P_GB:P_GB + LANES]) * jax.nn.silu(res[:, P_Z:P_Z + LANES])
            qs[j] = _l2norm(jax.nn.silu(conv(P_Q))) * (GDN_DK ** -0.5)
            ks[j] = _l2norm(jax.nn.silu(conv(P_K)))
            vs[j] = jax.nn.silu(conv(P_V))

        c0, c1 = j0 * CONV_GW, (j0 + heads_per_pass) * CONV_GW
        cbuf[0:TAIL, c0:c1] = cbuf[tt:tt + TAIL, c0:c1]

        probs = [(j, n) for j in heads for n in range(nc)]
        gcol = {p: sl(cols, p[1])[:, L_G + p[0]:L_G + p[0] + 1] for p in probs}
        egc = {p: sl(cols, p[1])[:, L_EG + p[0]:L_EG + p[0] + 1] for p in probs}
        bcol = {p: sl(cols, p[1])[:, L_BETA + p[0]:L_BETA + p[0] + 1] for p in probs}
        decay = {p: jnp.exp(jnp.where(causal, gcol[p] - g_t[p[0]:p[0] + 1, p[1] * chunk:(p[1] + 1) * chunk],
                                      0.0)) for p in probs}
        kn = {p: sl(ks[p[0]], p[1]) for p in probs}
        qn = {p: sl(qs[p[0]], p[1]) for p in probs}
        kb = {p: kn[p] * bcol[p] for p in probs}
        knb = {p: kn[p].astype(BF16) for p in probs}
        kk = {p: lax.dot_general(kb[p].astype(BF16), knb[p], (((1,), (1,)), ((), ())),
                                 preferred_element_type=F32) for p in probs}
        qk = {p: lax.dot_general(qn[p].astype(BF16), knb[p], (((1,), (1,)), ((), ())),
                                 preferred_element_type=F32) for p in probs}
        lows = [jnp.where(strict, kk[p] * decay[p], 0.0) for p in probs]
        a_intra = {p: jnp.where(causal, qk[p] * decay[p], 0.0).astype(BF16) for p in probs}
        tinv = dict(zip(probs, _tri_inverse_many(lows, chunk)))
        sol = {p: jnp.dot(tinv[p].astype(BF16),
                          jnp.concatenate([sl(vs[p[0]], p[1]) * bcol[p], kb[p] * egc[p]], axis=1).astype(BF16),
                          preferred_element_type=F32) for p in probs}
        g_last = {p: gcol[p][chunk - 1:chunk, :] for p in probs}
        wq = {p: jnp.concatenate([sol[p][:, GDN_DV:], qn[p] * egc[p]], axis=0).astype(BF16) for p in probs}
        k_tail = {p: (kn[p] * jnp.exp(g_last[p] - gcol[p])).astype(BF16) for p in probs}

        states = {j: s_sc[j] for j in heads}
        outs = {}
        for n in range(nc):
            ws = {j: jnp.dot(wq[(j, n)], states[j].astype(BF16), preferred_element_type=F32) for j in heads}
            v_new = {j: (sol[(j, n)][:, :GDN_DV] - ws[j][:chunk]).astype(BF16) for j in heads}
            for j in heads:
                outs[(j, n)] = ws[j][chunk:] + jnp.dot(a_intra[(j, n)], v_new[j], preferred_element_type=F32)
            states = {j: states[j] * jnp.exp(g_last[(j, n)])
                      + lax.dot_general(k_tail[(j, n)], v_new[j], (((0,), (0,)), ((), ())),
                                        preferred_element_type=F32) for j in heads}
        for j in heads:
            s_sc[j] = states[j]

        for j in heads:
            o = outs[(j, 0)] if nc == 1 else jnp.concatenate([outs[(j, n)] for n in range(nc)], axis=0)
            merged.append((gated_a[j] + _rms_gate(o, nw_ref[...], zgs[j])).astype(BF16))

    m = jnp.concatenate(merged, axis=1)
    y = jnp.dot(m, wo_ref[...], preferred_element_type=F32)
    out_ref[0] = _layer_norm(alpha * x + y, g2_ref[...], b2_ref[...])

    @pl.when(t == nt - 1)
    def _():
        h_ref[0] = h_sc[...]
        for j in range(N_GROUPS):
            lo = j * LANES
            rc_ref[0, :, lo:lo + LANES] = cbuf[HIST:TAIL, j * CONV_GW + P_RGX:j * CONV_GW + P_RGX + LANES]
            for c, p in enumerate((P_Q, P_K, P_V)):
                gc_ref[0, :, c * GDN_QK + lo:c * GDN_QK + lo + LANES] = (
                    cbuf[HIST:TAIL, j * CONV_GW + p:j * CONV_GW + p + LANES])
        s_ref[0] = s_sc[...]


def _layer_prompt(x, h0, rc0, s0, gc0, lp, *, alpha, tt, chunk):
    bsz, seq, _ = x.shape
    nt = seq // tt
    per_b3 = lambda b, t: (b, 0, 0)
    per_b4 = lambda b, t: (b, 0, 0, 0)
    kern = functools.partial(_layer_prompt_kernel, tt=tt, chunk=chunk, alpha=alpha,
                             heads_per_pass=N_GROUPS // 2)
    return pl.pallas_call(
        kern,
        grid=(bsz, nt),
        in_specs=[pl.BlockSpec((1, tt, D_MODEL), lambda b, t: (b, t, 0)),
                  pl.BlockSpec((1, 1, D_RNN), per_b3),
                  pl.BlockSpec((1, CONV_W - 1, D_RNN), per_b3),
                  pl.BlockSpec((1, GDN_HEADS, GDN_DK, GDN_DV), per_b4),
                  pl.BlockSpec((1, CONV_W - 1, 3 * GDN_QK), per_b3),
                  _const_spec((D_MODEL, D_MAIN)), _const_spec((1, D_MAIN)),
                  _const_spec((D_MODEL, LANES)), _const_spec((1, LANES)),
                  _const_spec((CONV_W, CONV_CH)), _const_spec((1, D_RNN)),
                  _const_spec((N_GROUPS, LANES, 2 * LANES)),
                  _const_spec((1, D_RNN)), _const_spec((1, D_RNN)), _const_spec((N_GROUPS, LANES)),
                  _const_spec((GDN_HEADS, 1)), _const_spec((GDN_HEADS, 1)), _const_spec((1, GDN_DV)),
                  _const_spec((D_MODEL, D_MODEL)), _const_spec((1, D_MODEL)), _const_spec((1, D_MODEL))],
        out_specs=[pl.BlockSpec((1, tt, D_MODEL), lambda b, t: (b, t, 0)),
                   pl.BlockSpec((1, 1, D_RNN), per_b3),
                   pl.BlockSpec((1, CONV_W - 1, D_RNN), per_b3),
                   pl.BlockSpec((1, GDN_HEADS, GDN_DK, GDN_DV), per_b4),
                   pl.BlockSpec((1, CONV_W - 1, 3 * GDN_QK), per_b3)],
        out_shape=[jax.ShapeDtypeStruct((bsz, seq, D_MODEL), F32),
                   jax.ShapeDtypeStruct((bsz, 1, D_RNN), F32),
                   jax.ShapeDtypeStruct((bsz, CONV_W - 1, D_RNN), F32),
                   jax.ShapeDtypeStruct((bsz, GDN_HEADS, GDN_DK, GDN_DV), F32),
                   jax.ShapeDtypeStruct((bsz, CONV_W - 1, 3 * GDN_QK), F32)],
        scratch_shapes=[pltpu.VMEM((tt + TAIL, CONV_CH), F32),
                        pltpu.VMEM((1, D_RNN), F32),
                        pltpu.VMEM((GDN_HEADS, GDN_DK, GDN_DV), F32)],
        compiler_params=pltpu.CompilerParams(dimension_semantics=("arbitrary", "arbitrary"),
                                             vmem_limit_bytes=VMEM_LIMIT),
        name="mixer_prompt",
    )(x, h0, rc0, s0, gc0,
      lp["w_main"], lp["b_main"], lp["w_ab"], lp["b_ab"],
      lp["cw"], lp["cb"], lp["wg"], lp["br"], lp["bi"], lp["lam"],
      lp["alog_c"], lp["dtb_c"], lp["nw"], lp["w_o"], lp["ln2_g"], lp["ln2_b"])


def _mixer_sample_kernel(main_ref, ab_ref, h0_ref, rc0_ref, s0_ref, gc0_ref,
                         cw_ref, cb_ref, wg_ref, br_ref, bi_ref, lam_ref,
                         alog_ref, dtb_ref, nw_ref, *refs, bb, has_acc):
    merged_ref, h_ref, rc_ref, s_ref, gc_ref = refs[1:] if has_acc else refs
    nprev = CONV_W - 1

    def conv(hist_ref, ch, off_hist, j, p):
        cc = j * CONV_GW + p
        acc = cw_ref[nprev:CONV_W, cc:cc + LANES] * main_ref[:, j * GROUP_W + p:j * GROUP_W + p + LANES]
        for i in range(nprev):
            acc = acc + (cw_ref[i:i + 1, cc:cc + LANES]
                         * hist_ref[:, i * ch + off_hist:i * ch + off_hist + LANES])
        return acc

    def part(j, p):
        return main_ref[:, j * GROUP_W + p:j * GROUP_W + p + LANES]

    qkv_w = 3 * GDN_QK
    rc_ref[:, 0:2 * D_RNN] = rc0_ref[:, D_RNN:3 * D_RNN]
    gc_ref[:, 0:2 * qkv_w] = gc0_ref[:, qkv_w:3 * qkv_w]

    g, beta = _gdn_gates(ab_ref[...], alog_ref[...], dtb_ref[...])
    eg = jnp.exp(g)
    c_lam_all = -RG_C * _softplus(-lam_ref[...])

    for j in range(N_GROUPS):
        lo = j * LANES
        rc_ref[:, 2 * D_RNN + lo:2 * D_RNN + lo + LANES] = part(j, P_RGX)
        for c, p in enumerate((P_Q, P_K, P_V)):
            gc_ref[:, 2 * qkv_w + c * GDN_QK + lo:2 * qkv_w + c * GDN_QK + lo + LANES] = part(j, p)

        xc = conv(rc0_ref, D_RNN, lo, j, P_RGX) + cb_ref[:, lo:lo + LANES]
        a, bt = _rg_gates(xc, wg_ref[j], br_ref[:, lo:lo + LANES], bi_ref[:, lo:lo + LANES],
                          c_lam_all[j:j + 1, :])
        hseq = a * h0_ref[:, lo:lo + LANES] + bt
        h_ref[:, lo:lo + LANES] = hseq

        qn = _l2norm(jax.nn.silu(conv(gc0_ref, qkv_w, lo, j, P_Q))) * (GDN_DK ** -0.5)
        kn = _l2norm(jax.nn.silu(conv(gc0_ref, qkv_w, GDN_QK + lo, j, P_K)))
        v = jax.nn.silu(conv(gc0_ref, qkv_w, 2 * GDN_QK + lo, j, P_V))
        qn_t = qn.T
        kn_t = kn.T
        o_rows = []
        for b in range(bb):
            s = s0_ref[b, j]
            kcol = kn_t[:, b:b + 1]
            egb = eg[b:b + 1, j:j + 1]
            ks = jnp.sum(s * kcol, axis=0, keepdims=True)
            v_new = beta[b:b + 1, GDN_HEADS + j:GDN_HEADS + j + 1] * (v[b:b + 1, :] - egb * ks)
            s_new = s * egb + kcol * v_new
            s_ref[b, j] = s_new
            o_rows.append(jnp.sum(s_new * qn_t[:, b:b + 1], axis=0, keepdims=True))
        o = jnp.concatenate(o_rows, axis=0)
        gated_a = jax.nn.sigmoid(part(j, P_GA)) * (hseq * jax.nn.gelu(part(j, P_RGY)))
        zg = jax.nn.sigmoid(part(j, P_GB)) * jax.nn.silu(part(j, P_Z))
        merged_ref[:, lo:lo + LANES] = (gated_a + _rms_gate(o, nw_ref[...], zg)).astype(BF16)


def _mixer_sample(main, ab, h0, rc0, s_all, gc0, lp, *, layer, s_acc, bb):
    bsz = main.shape[0]
    row2 = lambda i: (i, 0)
    s_spec = pl.BlockSpec((None, bb, GDN_HEADS, GDN_DK, GDN_DV), lambda i: (layer, i, 0, 0, 0))
    has_acc = s_acc is not None
    kern = functools.partial(_mixer_sample_kernel, bb=bb, has_acc=has_acc)
    qkv3 = 3 * 3 * GDN_QK
    in_specs = [pl.BlockSpec((bb, D_MAIN), row2),
                pl.BlockSpec((bb, LANES), row2),
                pl.BlockSpec((bb, D_RNN), row2),
                pl.BlockSpec((bb, 3 * D_RNN), row2),
                s_spec,
                pl.BlockSpec((bb, qkv3), row2),
                _const_spec((CONV_W, CONV_CH)), _const_spec((1, D_RNN)),
                _const_spec((N_GROUPS, LANES, 2 * LANES)),
                _const_spec((1, D_RNN)), _const_spec((1, D_RNN)), _const_spec((N_GROUPS, LANES)),
                _const_spec((1, LANES)), _const_spec((1, LANES)), _const_spec((1, GDN_DV))]
    args = [main, ab, h0, rc0, s_all, gc0, lp["cw"], lp["cb"], lp["wg"], lp["br"], lp["bi"], lp["lam"],
            lp["alog"], lp["dtb"], lp["nw"]]
    aliases = {}
    if has_acc:
        in_specs.append(pl.BlockSpec(memory_space=pl.ANY))
        args.append(s_acc)
        aliases = {len(args) - 1: 3}
    return pl.pallas_call(
        kern,
        grid=(bsz // bb,),
        in_specs=in_specs,
        out_specs=[pl.BlockSpec((bb, D_MODEL), row2),
                   pl.BlockSpec((bb, D_RNN), row2),
                   pl.BlockSpec((bb, 3 * D_RNN), row2),
                   s_spec,
                   pl.BlockSpec((bb, qkv3), row2)],
        out_shape=[jax.ShapeDtypeStruct((bsz, D_MODEL), BF16),
                   jax.ShapeDtypeStruct((bsz, D_RNN), F32),
                   jax.ShapeDtypeStruct((bsz, 3 * D_RNN), F32),
                   jax.ShapeDtypeStruct(s_all.shape, F32),
                   jax.ShapeDtypeStruct((bsz, qkv3), F32)],
        input_output_aliases=aliases,
        compiler_params=pltpu.CompilerParams(dimension_semantics=("arbitrary",),
                                             vmem_limit_bytes=VMEM_LIMIT),
        name="mixer_sample",
    )(*args)


def _prep_layer(p):
    (ln1_g, ln1_b, ffn1_w1, ffn1_w3, ffn1_w2, w_in, b_in, rg_conv_w, rg_conv_b, rg_wr, rg_br,
     rg_wi, rg_bi, rg_lambda, gdn_conv_w, gdn_a_log, gdn_dt_bias, gdn_norm_w, w_o, ln2_g, ln2_b,
     ffn2_w1, ffn2_w3, ffn2_w2, ln3_g, ln3_b) = p
    row = lambda v: v.reshape(1, -1)

    def reorder(w):
        a0 = 2 * D_RNN + 2 * GDN_QK + 2 * GDN_VW
        g0 = a0 + 2 * GDN_HEADS
        starts = (0, 2 * D_RNN, 3 * D_RNN, 4 * D_RNN, D_RNN, 5 * D_RNN, g0, g0 + D_MODEL)
        pieces = [w[..., s:s + D_RNN].reshape(w.shape[:-1] + (N_GROUPS, 1, LANES)) for s in starts]
        main = jnp.concatenate(pieces, axis=-2).reshape(w.shape[:-1] + (D_MAIN,))
        ab = w[..., a0:g0]
        ab = jnp.pad(ab, [(0, 0)] * (w.ndim - 1) + [(0, LANES - 2 * GDN_HEADS)])
        return main, ab

    w_main, w_ab = reorder(w_in)
    b_main, b_ab = reorder(row(b_in))

    def pair_blocks(w):
        w = w.reshape(N_GROUPS, 2, RG_BLOCK, RG_BLOCK)
        z = jnp.zeros((N_GROUPS, RG_BLOCK, RG_BLOCK), w.dtype)
        top = jnp.concatenate([w[:, 0], z], axis=2)
        bot = jnp.concatenate([z, w[:, 1]], axis=2)
        return jnp.concatenate([top, bot], axis=1)

    wg = jnp.concatenate([pair_blocks(rg_wr), pair_blocks(rg_wi)], axis=2).astype(BF16)
    cw = jnp.concatenate([rg_conv_w.reshape(CONV_W, N_GROUPS, 1, LANES),
                          gdn_conv_w.reshape(CONV_W, 3, N_GROUPS, LANES).transpose(0, 2, 1, 3)],
                         axis=2).reshape(CONV_W, CONV_CH)
    pad_h = lambda v: jnp.pad(row(v), [(0, 0), (0, LANES - GDN_HEADS)])
    return dict(
        ln1_g=row(ln1_g), ln1_b=row(ln1_b), ln2_g=row(ln2_g), ln2_b=row(ln2_b),
        ln3_g=row(ln3_g), ln3_b=row(ln3_b),
        f1=(ffn1_w1.astype(BF16), ffn1_w3.astype(BF16), ffn1_w2.astype(BF16)),
        f2=(ffn2_w1.astype(BF16), ffn2_w3.astype(BF16), ffn2_w2.astype(BF16)),
        w_main=w_main.astype(BF16), b_main=b_main, w_ab=w_ab.astype(BF16), b_ab=b_ab,
        w_o=w_o.astype(BF16),
        cw=cw, cb=row(rg_conv_b),
        wg=wg, br=row(rg_br), bi=row(rg_bi), lam=rg_lambda.reshape(N_GROUPS, LANES),
        alog=pad_h(gdn_a_log), dtb=pad_h(gdn_dt_bias),
        alog_c=gdn_a_log.reshape(GDN_HEADS, 1), dtb_c=gdn_dt_bias.reshape(GDN_HEADS, 1),
        nw=row(gdn_norm_w))


def _trunk_prompt(x, layers, *, alpha, tm, tt, chunk):
    bsz, seq, _ = x.shape
    h0 = jnp.zeros((bsz, 1, D_RNN), x.dtype)
    rc0 = jnp.zeros((bsz, CONV_W - 1, D_RNN), x.dtype)
    s0 = jnp.zeros((bsz, GDN_HEADS, GDN_DK, GDN_DV), x.dtype)
    gc0 = jnp.zeros((bsz, CONV_W - 1, 3 * GDN_QK), x.dtype)
    hs, rcs, ss, gcs = [], [], [], []
    for lp in layers:
        x1 = _ffn_ln(x.reshape(bsz * seq, D_MODEL), *lp["f1"], lp["ln1_g"], lp["ln1_b"],
                     alpha=alpha, tm=tm)
        x2, h, rc, s, gc = _layer_prompt(x1.reshape(bsz, seq, D_MODEL), h0, rc0, s0, gc0, lp,
                                         alpha=alpha, tt=min(tt, seq), chunk=chunk)
        x = _ffn_ln(x2.reshape(bsz * seq, D_MODEL), *lp["f2"], lp["ln3_g"], lp["ln3_b"],
                    alpha=alpha, tm=tm).reshape(bsz, seq, D_MODEL)
        hs.append(h.reshape(bsz, D_RNN))
        rcs.append(rc)
        ss.append(s)
        gcs.append(gc)
    return x, jnp.stack(hs), jnp.stack(rcs), jnp.stack(ss), jnp.stack(gcs)


def _trunk_sample(x, h0, rc0, s0, gc0, layers, *, alpha, bb):
    bsz = x.shape[0]
    xf = x.reshape(bsz, D_MODEL)
    hs, rcs, gcs = [], [], []
    s_acc = None
    for l, lp in enumerate(layers):
        xf = _ffn_ln(xf, *lp["f1"], lp["ln1_g"], lp["ln1_b"], alpha=alpha, tm=bsz)
        main, ab = _in_proj(xf, lp["w_main"], lp["b_main"], lp["w_ab"], lp["b_ab"], tm=bsz)
        merged, h, rc, s_acc, gc = _mixer_sample(
            main, ab, h0[l], rc0[l].reshape(bsz, -1), s0, gc0[l].reshape(bsz, -1), lp,
            layer=l, s_acc=s_acc, bb=bb)
        xf = _out_proj_ln(xf, merged, lp["w_o"], lp["ln2_g"], lp["ln2_b"], alpha=alpha, tm=bsz)
        xf = _ffn_ln(xf, *lp["f2"], lp["ln3_g"], lp["ln3_b"], alpha=alpha, tm=bsz)
        hs.append(h)
        rcs.append(rc.reshape(bsz, CONV_W - 1, D_RNN))
        gcs.append(gc.reshape(bsz, CONV_W - 1, 3 * GDN_QK))
    return xf.reshape(bsz, 1, D_MODEL), jnp.stack(hs), jnp.stack(rcs), s_acc, jnp.stack(gcs)


def kernel(x_prompt, x_sample, state_rglru_h, state_rglru_conv, state_gdn_S, state_gdn_conv, ln1_g, ln1_b, ffn1_w1, ffn1_w3, ffn1_w2, w_in, b_in, rg_conv_w, rg_conv_b, rg_wr, rg_br, rg_wi, rg_bi, rg_lambda, gdn_conv_w, gdn_a_log, gdn_dt_bias, gdn_norm_w, w_o, ln2_g, ln2_b, ffn2_w1, ffn2_w3, ffn2_w2, ln3_g, ln3_b):
    params = (ln1_g, ln1_b, ffn1_w1, ffn1_w3, ffn1_w2, w_in, b_in, rg_conv_w, rg_conv_b, rg_wr, rg_br,
              rg_wi, rg_bi, rg_lambda, gdn_conv_w, gdn_a_log, gdn_dt_bias, gdn_norm_w, w_o, ln2_g, ln2_b,
              ffn2_w1, ffn2_w3, ffn2_w2, ln3_g, ln3_b)
    layers = [_prep_layer([p[l] for p in params]) for l in range(ln1_g.shape[0])]
    y_p, p_h, p_rc, p_s, p_gc = _trunk_prompt(x_prompt, layers, alpha=ALPHA, tm=512, tt=256, chunk=128)
    y_s, s_h, s_rc, s_s, s_gc = _trunk_sample(
        x_sample, state_rglru_h, state_rglru_conv, state_gdn_S, state_gdn_conv, layers,
        alpha=ALPHA, bb=8)
    return (y_p, y_s, p_h, p_rc, p_s, p_gc, s_h, s_rc, s_s, s_gc)
```

```python
import functools

import jax
import jax.numpy as jnp
from jax import lax
from jax.experimental import pallas as pl
from jax.experimental.pallas import tpu as pltpu

F32 = jnp.float32
BF16 = jnp.bfloat16

D_MODEL = 1024
DEPTH = 4
ALPHA = (2 * DEPTH) ** 0.25
D_RNN = D_MODEL
RG_BLOCK = 64
RG_C = 8.0
CONV_W = 4
GDN_HEADS = 8
GDN_DK = 128
GDN_DV = 128
GDN_QK = GDN_HEADS * GDN_DK
GDN_VW = GDN_HEADS * GDN_DV
D_FF = 2816
LN_EPS = 1e-5
RMS_EPS = 1e-6
LANES = 128
SUBLANES = 8
N_GROUPS = D_RNN // LANES

P_RGX, P_Q, P_K, P_V, P_RGY, P_Z, P_GA, P_GB = (i * LANES for i in range(8))
GROUP_W = 8 * LANES
CONV_GW = 4 * LANES
CONV_CH = N_GROUPS * CONV_GW
D_MAIN = N_GROUPS * GROUP_W
TAIL = SUBLANES
HIST = TAIL - (CONV_W - 1)
INV_BASE = 16
FF_CHUNK = 256

VMEM_LIMIT = 56 * 1024 * 1024


def _mm(a, b):
    return jnp.dot(a.astype(BF16), b.astype(BF16), preferred_element_type=F32)


def _softplus(x):
    return jnp.maximum(x, 0.0) + jnp.log1p(jnp.exp(-jnp.abs(x)))


def _layer_norm(r, g, b):
    mu = jnp.mean(r, axis=-1, keepdims=True)
    c = r - mu
    var = jnp.mean(c * c, axis=-1, keepdims=True)
    return c * lax.rsqrt(var + LN_EPS) * g + b


def _ffn_ln_math(x, w1_ref, w3_ref, w2_ref, g, b, *, alpha, ff_chunk):
    xb = x.astype(BF16)
    acc = None
    pending = None
    for c in range(D_FF // ff_chunk):
        sl = slice(c * ff_chunk, (c + 1) * ff_chunk)
        h1 = jnp.dot(xb, w1_ref[:, sl], preferred_element_type=F32)
        h3 = jnp.dot(xb, w3_ref[:, sl], preferred_element_type=F32)
        if pending is not None:
            y = jnp.dot(pending[0], w2_ref[pending[1], :], preferred_element_type=F32)
            acc = y if acc is None else acc + y
        pending = ((jax.nn.silu(h1) * h3).astype(BF16), sl)
    y = jnp.dot(pending[0], w2_ref[pending[1], :], preferred_element_type=F32)
    acc = y if acc is None else acc + y
    return _layer_norm(alpha * x + 0.5 * acc, g, b)


def _ffn_ln_kernel(x_ref, w1_ref, w3_ref, w2_ref, g_ref, b_ref, o_ref, *, alpha, ff_chunk):
    o_ref[...] = _ffn_ln_math(x_ref[...], w1_ref, w3_ref, w2_ref, g_ref[...], b_ref[...],
                              alpha=alpha, ff_chunk=ff_chunk)


def _layer_spec(layer, shape):
    nd = len(shape)
    return pl.BlockSpec((None,) + tuple(shape), lambda *_: (layer,) + (0,) * nd,
                        pipeline_mode=pl.Buffered(1))


def _ffn_ln(x, w1, w3, w2, g, b, *, layer, alpha, tm):
    m = x.shape[0]
    cs = functools.partial(_layer_spec, layer)
    return pl.pallas_call(
        functools.partial(_ffn_ln_kernel, alpha=alpha, ff_chunk=FF_CHUNK),
        grid=(m // tm,),
        in_specs=[pl.BlockSpec((tm, D_MODEL), lambda i: (i, 0)),
                  cs((D_MODEL, D_FF)), cs((D_MODEL, D_FF)),
                  cs((D_FF, D_MODEL)),
                  cs((1, D_MODEL)), cs((1, D_MODEL))],
        out_specs=pl.BlockSpec((tm, D_MODEL), lambda i: (i, 0)),
        out_shape=jax.ShapeDtypeStruct((m, D_MODEL), F32),
        compiler_params=pltpu.CompilerParams(dimension_semantics=("arbitrary",),
                                             vmem_limit_bytes=VMEM_LIMIT),
        name="ffn_ln",
    )(x, w1, w3, w2, g, b)


def _in_proj_kernel(x_ref, w_ref, b_ref, wab_ref, bab_ref, main_ref, ab_ref, *, n_chunk):
    xb = x_ref[...].astype(BF16)
    for c in range(D_MAIN // n_chunk):
        sl = slice(c * n_chunk, (c + 1) * n_chunk)
        main_ref[:, sl] = jnp.dot(xb, w_ref[:, sl], preferred_element_type=F32) + b_ref[:, sl]
    ab_ref[...] = jnp.dot(xb, wab_ref[...], preferred_element_type=F32) + bab_ref[...]


def _in_proj(x, w_main, b_main, w_ab, b_ab, *, layer, tm):
    m = x.shape[0]
    cs = functools.partial(_layer_spec, layer)
    return pl.pallas_call(
        functools.partial(_in_proj_kernel, n_chunk=GROUP_W),
        grid=(m // tm,),
        in_specs=[pl.BlockSpec((tm, D_MODEL), lambda i: (i, 0)),
                  cs((D_MODEL, D_MAIN)), cs((1, D_MAIN)),
                  cs((D_MODEL, LANES)), cs((1, LANES))],
        out_specs=[pl.BlockSpec((tm, D_MAIN), lambda i: (i, 0)),
                   pl.BlockSpec((tm, LANES), lambda i: (i, 0))],
        out_shape=[jax.ShapeDtypeStruct((m, D_MAIN), F32),
                   jax.ShapeDtypeStruct((m, LANES), F32)],
        compiler_params=pltpu.CompilerParams(dimension_semantics=("arbitrary",),
                                             vmem_limit_bytes=VMEM_LIMIT),
        name="in_proj",
    )(x, w_main, b_main, w_ab, b_ab)


def _out_proj_ln_kernel(x_ref, m_ref, w_ref, g_ref, b_ref, o_ref, *, alpha):
    y = jnp.dot(m_ref[...], w_ref[...], preferred_element_type=F32)
    o_ref[...] = _layer_norm(alpha * x_ref[...] + y, g_ref[...], b_ref[...])


def _out_proj_ln(x, merged, w_o, g, b, *, layer, alpha, tm):
    m = x.shape[0]
    cs = functools.partial(_layer_spec, layer)
    return pl.pallas_call(
        functools.partial(_out_proj_ln_kernel, alpha=alpha),
        grid=(m // tm,),
        in_specs=[pl.BlockSpec((tm, D_MODEL), lambda i: (i, 0)),
                  pl.BlockSpec((tm, D_MODEL), lambda i: (i, 0)),
                  cs((D_MODEL, D_MODEL)),
                  cs((1, D_MODEL)), cs((1, D_MODEL))],
        out_specs=pl.BlockSpec((tm, D_MODEL), lambda i: (i, 0)),
        out_shape=jax.ShapeDtypeStruct((m, D_MODEL), F32),
        compiler_params=pltpu.CompilerParams(dimension_semantics=("arbitrary",),
                                             vmem_limit_bytes=VMEM_LIMIT),
        name="out_proj_ln",
    )(x, merged, w_o, g, b)


def _rg_gates(xc, wg, br, bi, c_lam):
    pre = _mm(xc, wg)
    r = jax.nn.sigmoid(pre[:, :LANES] + br)
    i = jax.nn.sigmoid(pre[:, LANES:] + bi)
    log_a = c_lam * r
    a = jnp.exp(log_a)
    bt = jnp.sqrt(1.0 - a * a) * (i * xc)
    return a, bt


def _l2norm(t):
    return t * lax.rsqrt(jnp.sum(t * t, axis=-1, keepdims=True) + RMS_EPS)


def _rms_gate(o, nw, zg):
    return o * lax.rsqrt(jnp.mean(o * o, axis=-1, keepdims=True) + RMS_EPS) * nw * zg


def _gdn_gates(ab, alog, dtb):
    g = -jnp.exp(alog) * _softplus(ab + dtb)
    beta = jax.nn.sigmoid(ab)
    return g, beta


def _rg_scan(a, b, h_prev):
    n = a.shape[0]
    nv = n // SUBLANES
    a3 = a.reshape(nv, SUBLANES, LANES)
    b3 = b.reshape(nv, SUBLANES, LANES)
    sub = lax.broadcasted_iota(jnp.int32, (nv, SUBLANES, LANES), 1)
    d = 1
    while d < SUBLANES:
        keep = sub >= d
        a_sh = jnp.where(keep, pltpu.roll(a3, d, axis=1), 1.0)
        b_sh = jnp.where(keep, pltpu.roll(b3, d, axis=1), 0.0)
        b3 = a3 * b_sh + b3
        a3 = a3 * a_sh
        d *= 2
    carry = h_prev
    hs = []
    for v in range(nv):
        hv = a3[v] * carry + b3[v]
        hs.append(hv)
        carry = hv[SUBLANES - 1:SUBLANES, :]
    return jnp.concatenate(hs, axis=0), carry


def _tri_inverse_many(lows, c):
    row = lax.broadcasted_iota(jnp.int32, (c, c), 0)
    col = lax.broadcasted_iota(jnp.int32, (c, c), 1)
    eye = (row == col).astype(F32)
    same = (row // INV_BASE) == (col // INV_BASE)

    def left_products(acc, pw, exp, limit):
        while exp < limit:
            if 2 * exp < limit:
                rhs = [jnp.concatenate([p, a.astype(BF16)], axis=1) for p, a in zip(pw, acc)]
                out = [jnp.dot(p, r, preferred_element_type=F32) for p, r in zip(pw, rhs)]
                pw = [o[:, :c].astype(BF16) for o in out]
                acc = [a + o[:, c:] for a, o in zip(acc, out)]
            else:
                acc = [a + jnp.dot(p, a.astype(BF16), preferred_element_type=F32) for p, a in zip(pw, acc)]
            exp *= 2
        return acc

    nd = [(-jnp.where(same, low, 0.0)) for low in lows]
    ndb = [x.astype(BF16) for x in nd]
    xs = [eye + x for x in nd]
    if INV_BASE > 2:
        sq = [jnp.dot(x, x, preferred_element_type=F32).astype(BF16) for x in ndb]
        xs = left_products(xs, sq, 2, INV_BASE)
    n_blocks = c // INV_BASE
    if n_blocks == 1:
        return xs
    xbs = [x.astype(BF16) for x in xs]
    ms = [jnp.dot(xb, jnp.where(same, 0.0, low).astype(BF16), preferred_element_type=F32).astype(BF16)
          for xb, low in zip(xbs, lows)]
    if n_blocks == 2:
        return [x - jnp.dot(m, xb, preferred_element_type=F32) for x, m, xb in zip(xs, ms, xbs)]
    out = [jnp.dot(m, jnp.concatenate([xb, m], axis=1), preferred_element_type=F32)
           for m, xb in zip(ms, xbs)]
    zs = [x - o[:, :c] for x, o in zip(xs, out)]
    m2 = [o[:, c:].astype(BF16) for o in out]
    return left_products(zs, m2, 2, n_blocks)


def _layer_prompt_kernel(x_ref, h0_ref, rc0_ref, s0_ref, gc0_ref,
                         wm_ref, bm_ref, wab_ref, bab_ref,
                         cw_ref, cb_ref, wg_ref, br_ref, bi_ref, lam_ref,
                         alog_ref, dtb_ref, nw_ref, wo_ref, g2_ref, b2_ref,
                         out_ref, h_ref, rc_ref, s_ref, gc_ref,
                         cbuf, h_sc, s_sc, *, tt, chunk, alpha, heads_per_pass):
    t = pl.program_id(1)
    nt = pl.num_programs(1)
    nc = tt // chunk

    @pl.when(t == 0)
    def _():
        cbuf[0:TAIL, :] = jnp.zeros((TAIL, CONV_CH), F32)
        for j in range(N_GROUPS):
            lo = j * LANES
            cbuf[HIST:TAIL, j * CONV_GW + P_RGX:j * CONV_GW + P_RGX + LANES] = rc0_ref[0, :, lo:lo + LANES]
            for c, p in enumerate((P_Q, P_K, P_V)):
                cbuf[HIST:TAIL, j * CONV_GW + p:j * CONV_GW + p + LANES] = (
                    gc0_ref[0, :, c * GDN_QK + lo:c * GDN_QK + lo + LANES])
        h_sc[...] = h0_ref[0]
        s_sc[...] = s0_ref[0]

    x = x_ref[0]
    xb = x.astype(BF16)
    ab_t = (jnp.dot(xb, wab_ref[...], preferred_element_type=F32) + bab_ref[...]).T
    g_t = -jnp.exp(alog_ref[...]) * _softplus(ab_t[0:GDN_HEADS] + dtb_ref[...])
    beta_t = jax.nn.sigmoid(ab_t[GDN_HEADS:2 * GDN_HEADS])
    lin = lax.broadcasted_iota(jnp.int32, (GDN_HEADS, tt), 1) % chunk
    d = 1
    while d < chunk:
        g_t = g_t + jnp.where(lin >= d, pltpu.roll(g_t, d, axis=1), 0.0)
        d *= 2
    cols = jnp.concatenate([g_t, beta_t, jnp.exp(g_t),
                            jnp.zeros((LANES - 3 * GDN_HEADS, tt), F32)], axis=0).T
    L_G, L_BETA, L_EG = 0, GDN_HEADS, 2 * GDN_HEADS
    c_lam_all = -RG_C * _softplus(-lam_ref[...])

    rw = lax.broadcasted_iota(jnp.int32, (chunk, chunk), 0)
    cl = lax.broadcasted_iota(jnp.int32, (chunk, chunk), 1)
    causal = rw >= cl
    strict = rw > cl
    sl = lambda arr, n: arr[n * chunk:(n + 1) * chunk]
    merged = []
    for j0 in range(0, N_GROUPS, heads_per_pass):
        heads = range(j0, j0 + heads_per_pass)

        gated_a, zgs, qs, ks, vs = {}, {}, {}, {}, {}
        for j in heads:
            lo = j * LANES
            cg = j * CONV_GW
            res = (jnp.dot(xb, wm_ref[:, j * GROUP_W:(j + 1) * GROUP_W], preferred_element_type=F32)
                   + bm_ref[:, j * GROUP_W:(j + 1) * GROUP_W])
            cbuf[TAIL:TAIL + tt, cg:cg + CONV_GW] = res[:, 0:CONV_GW]

            def conv(p):
                acc = cw_ref[CONV_W - 1:CONV_W, cg + p:cg + p + LANES] * res[:, p:p + LANES]
                for i in range(CONV_W - 1):
                    acc = acc + (cw_ref[i:i + 1, cg + p:cg + p + LANES]
                                 * cbuf[HIST + i:HIST + i + tt, cg + p:cg + p + LANES])
                return acc

            xc = conv(P_RGX) + cb_ref[:, lo:lo + LANES]
            a, bt = _rg_gates(xc, wg_ref[j], br_ref[:, lo:lo + LANES], bi_ref[:, lo:lo + LANES],
                              c_lam_all[j:j + 1, :])
            hseq, h_last = _rg_scan(a, bt, h_sc[:, lo:lo + LANES])
            h_sc[:, lo:lo + LANES] = h_last
            gated_a[j] = (jax.nn.sigmoid(res[:, P_GA:P_GA + LANES])
                          * (hseq * jax.nn.gelu(res[:, P_RGY:P_RGY + LANES])))
            zgs[j] = jax.nn.sigmoid(res[:, P_GB:P_GB + LANES]) * jax.nn.silu(res[:, P_Z:P_Z + LANES])
            qs[j] = _l2norm(jax.nn.silu(conv(P_Q))) * (GDN_DK ** -0.5)
            ks[j] = _l2norm(jax.nn.silu(conv(P_K)))
            vs[j] = jax.nn.silu(conv(P_V))

        c0, c1 = j0 * CONV_GW, (j0 + heads_per_pass) * CONV_GW
        cbuf[0:TAIL, c0:c1] = cbuf[tt:tt + TAIL, c0:c1]

        probs = [(j, n) for j in heads for n in range(nc)]
        gcol = {p: sl(cols, p[1])[:, L_G + p[0]:L_G + p[0] + 1] for p in probs}
        egc = {p: sl(cols, p[1])[:, L_EG + p[0]:L_EG + p[0] + 1] for p in probs}
        bcol = {p: sl(cols, p[1])[:, L_BETA + p[0]:L_BETA + p[0] + 1] for p in probs}
        decay = {p: jnp.exp(jnp.where(causal, gcol[p] - g_t[p[0]:p[0] + 1, p[1] * chunk:(p[1] + 1) * chunk],
                                      0.0)) for p in probs}
        kn = {p: sl(ks[p[0]], p[1]) for p in probs}
        qn = {p: sl(qs[p[0]], p[1]) for p in probs}
        kb = {p: kn[p] * bcol[p] for p in probs}
        knb = {p: kn[p].astype(BF16) for p in probs}
        kk = {p: lax.dot_general(kb[p].astype(BF16), knb[p], (((1,), (1,)), ((), ())),
                                 preferred_element_type=F32) for p in probs}
        qk = {p: lax.dot_general(qn[p].astype(BF16), knb[p], (((1,), (1,)), ((), ())),
                                 preferred_element_type=F32) for p in probs}
        lows = [jnp.where(strict, kk[p] * decay[p], 0.0) for p in probs]
        a_intra = {p: jnp.where(causal, qk[p] * decay[p], 0.0).astype(BF16) for p in probs}
        tinv = dict(zip(probs, _tri_inverse_many(lows, chunk)))
        sol = {p: jnp.dot(tinv[p].astype(BF16),
                          jnp.concatenate([sl(vs[p[0]], p[1]) * bcol[p], kb[p] * egc[p]], axis=1).astype(BF16),
                          preferred_element_type=F32) for p in probs}
        g_last = {p: gcol[p][chunk - 1:chunk, :] for p in probs}
        wq = {p: jnp.concatenate([sol[p][:, GDN_DV:], qn[p] * egc[p]], axis=0).astype(BF16) for p in probs}
        k_tail = {p: (kn[p] * jnp.exp(g_last[p] - gcol[p])).astype(BF16) for p in probs}

        states = {j: s_sc[j] for j in heads}
        outs = {}
        for n in range(nc):
            ws = {j: jnp.dot(wq[(j, n)], states[j].astype(BF16), preferred_element_type=F32) for j in heads}
            v_new = {j: (sol[(j, n)][:, :GDN_DV] - ws[j][:chunk]).astype(BF16) for j in heads}
            for j in heads:
                outs[(j, n)] = ws[j][chunk:] + jnp.dot(a_intra[(j, n)], v_new[j], preferred_element_type=F32)
            states = {j: states[j] * jnp.exp(g_last[(j, n)])
                      + lax.dot_general(k_tail[(j, n)], v_new[j], (((0,), (0,)), ((), ())),
                                        preferred_element_type=F32) for j in heads}
        for j in heads:
            s_sc[j] = states[j]

        for j in heads:
            o = outs[(j, 0)] if nc == 1 else jnp.concatenate([outs[(j, n)] for n in range(nc)], axis=0)
            merged.append((gated_a[j] + _rms_gate(o, nw_ref[...], zgs[j])).astype(BF16))

    m = jnp.concatenate(merged, axis=1)
    y = jnp.dot(m, wo_ref[...], preferred_element_type=F32)
    out_ref[0] = _layer_norm(alpha * x + y, g2_ref[...], b2_ref[...])

    @pl.when(t == nt - 1)
    def _():
        h_ref[0] = h_sc[...]
        for j in range(N_GROUPS):
            lo = j * LANES
            rc_ref[0, :, lo:lo + LANES] = cbuf[HIST:TAIL, j * CONV_GW + P_RGX:j * CONV_GW + P_RGX + LANES]
            for c, p in enumerate((P_Q, P_K, P_V)):
                gc_ref[0, :, c * GDN_QK + lo:c * GDN_QK + lo + LANES] = (
                    cbuf[HIST:TAIL, j * CONV_GW + p:j * CONV_GW + p + LANES])
        s_ref[0] = s_sc[...]


def _layer_prompt(x, h0, rc0, s0, gc0, lp, *, layer, alpha, tt, chunk):
    bsz, seq, _ = x.shape
    nt = seq // tt
    cs = functools.partial(_layer_spec, layer)
    per_b3 = lambda b, t: (b, 0, 0)
    per_b4 = lambda b, t: (b, 0, 0, 0)
    kern = functools.partial(_layer_prompt_kernel, tt=tt, chunk=chunk, alpha=alpha,
                             heads_per_pass=N_GROUPS // 2)
    return pl.pallas_call(
        kern,
        grid=(bsz, nt),
        in_specs=[pl.BlockSpec((1, tt, D_MODEL), lambda b, t: (b, t, 0)),
                  pl.BlockSpec((1, 1, D_RNN), per_b3),
                  pl.BlockSpec((1, CONV_W - 1, D_RNN), per_b3),
                  pl.BlockSpec((1, GDN_HEADS, GDN_DK, GDN_DV), per_b4),
                  pl.BlockSpec((1, CONV_W - 1, 3 * GDN_QK), per_b3),
                  cs((D_MODEL, D_MAIN)), cs((1, D_MAIN)),
                  cs((D_MODEL, LANES)), cs((1, LANES)),
                  cs((CONV_W, CONV_CH)), cs((1, D_RNN)),
                  cs((N_GROUPS, LANES, 2 * LANES)),
                  cs((1, D_RNN)), cs((1, D_RNN)), cs((N_GROUPS, LANES)),
                  cs((GDN_HEADS, 1)), cs((GDN_HEADS, 1)), cs((1, GDN_DV)),
                  cs((D_MODEL, D_MODEL)), cs((1, D_MODEL)), cs((1, D_MODEL))],
        out_specs=[pl.BlockSpec((1, tt, D_MODEL), lambda b, t: (b, t, 0)),
                   pl.BlockSpec((1, 1, D_RNN), per_b3),
                   pl.BlockSpec((1, CONV_W - 1, D_RNN), per_b3),
                   pl.BlockSpec((1, GDN_HEADS, GDN_DK, GDN_DV), per_b4),
                   pl.BlockSpec((1, CONV_W - 1, 3 * GDN_QK), per_b3)],
        out_shape=[jax.ShapeDtypeStruct((bsz, seq, D_MODEL), F32),
                   jax.ShapeDtypeStruct((bsz, 1, D_RNN), F32),
                   jax.ShapeDtypeStruct((bsz, CONV_W - 1, D_RNN), F32),
                   jax.ShapeDtypeStruct((bsz, GDN_HEADS, GDN_DK, GDN_DV), F32),
                   jax.ShapeDtypeStruct((bsz, CONV_W - 1, 3 * GDN_QK), F32)],
        scratch_shapes=[pltpu.VMEM((tt + TAIL, CONV_CH), F32),
                        pltpu.VMEM((1, D_RNN), F32),
                        pltpu.VMEM((GDN_HEADS, GDN_DK, GDN_DV), F32)],
        compiler_params=pltpu.CompilerParams(dimension_semantics=("arbitrary", "arbitrary"),
                                             vmem_limit_bytes=VMEM_LIMIT),
        name="mixer_prompt",
    )(x, h0, rc0, s0, gc0,
      lp["w_main"], lp["b_main"], lp["w_ab"], lp["b_ab"],
      lp["cw"], lp["cb"], lp["wg"], lp["br"], lp["bi"], lp["lam"],
      lp["alog_c"], lp["dtb_c"], lp["nw"], lp["w_o"], lp["ln2_g"], lp["ln2_b"])


def _mixer_sample_kernel(main_ref, ab_ref, h0_ref, rc0_ref, s0_ref, gc0_ref,
                         cw_ref, cb_ref, wg_ref, br_ref, bi_ref, lam_ref,
                         alog_ref, dtb_ref, nw_ref, *refs, bb, has_acc):
    merged_ref, h_ref, rc_ref, s_ref, gc_ref = refs[1:] if has_acc else refs
    nprev = CONV_W - 1

    def conv(hist_ref, ch, off_hist, j, p):
        cc = j * CONV_GW + p
        acc = cw_ref[nprev:CONV_W, cc:cc + LANES] * main_ref[:, j * GROUP_W + p:j * GROUP_W + p + LANES]
        for i in range(nprev):
            acc = acc + (cw_ref[i:i + 1, cc:cc + LANES]
                         * hist_ref[:, i * ch + off_hist:i * ch + off_hist + LANES])
        return acc

    def part(j, p):
        return main_ref[:, j * GROUP_W + p:j * GROUP_W + p + LANES]

    qkv_w = 3 * GDN_QK
    rc_ref[:, 0:2 * D_RNN] = rc0_ref[:, D_RNN:3 * D_RNN]
    gc_ref[:, 0:2 * qkv_w] = gc0_ref[:, qkv_w:3 * qkv_w]

    g, beta = _gdn_gates(ab_ref[...], alog_ref[...], dtb_ref[...])
    eg = jnp.exp(g)
    c_lam_all = -RG_C * _softplus(-lam_ref[...])

    for j in range(N_GROUPS):
        lo = j * LANES
        rc_ref[:, 2 * D_RNN + lo:2 * D_RNN + lo + LANES] = part(j, P_RGX)
        for c, p in enumerate((P_Q, P_K, P_V)):
            gc_ref[:, 2 * qkv_w + c * GDN_QK + lo:2 * qkv_w + c * GDN_QK + lo + LANES] = part(j, p)

        xc = conv(rc0_ref, D_RNN, lo, j, P_RGX) + cb_ref[:, lo:lo + LANES]
        a, bt = _rg_gates(xc, wg_ref[j], br_ref[:, lo:lo + LANES], bi_ref[:, lo:lo + LANES],
                          c_lam_all[j:j + 1, :])
        hseq = a * h0_ref[:, lo:lo + LANES] + bt
        h_ref[:, lo:lo + LANES] = hseq

        qn = _l2norm(jax.nn.silu(conv(gc0_ref, qkv_w, lo, j, P_Q))) * (GDN_DK ** -0.5)
        kn = _l2norm(jax.nn.silu(conv(gc0_ref, qkv_w, GDN_QK + lo, j, P_K)))
        v = jax.nn.silu(conv(gc0_ref, qkv_w, 2 * GDN_QK + lo, j, P_V))
        qn_t = qn.T
        kn_t = kn.T
        o_rows = []
        for b in range(bb):
            s = s0_ref[b, j]
            kcol = kn_t[:, b:b + 1]
            egb = eg[b:b + 1, j:j + 1]
            ks = jnp.sum(s * kcol, axis=0, keepdims=True)
            v_new = beta[b:b + 1, GDN_HEADS + j:GDN_HEADS + j + 1] * (v[b:b + 1, :] - egb * ks)
            s_new = s * egb + kcol * v_new
            s_ref[b, j] = s_new
            o_rows.append(jnp.sum(s_new * qn_t[:, b:b + 1], axis=0, keepdims=True))
        o = jnp.concatenate(o_rows, axis=0)
        gated_a = jax.nn.sigmoid(part(j, P_GA)) * (hseq * jax.nn.gelu(part(j, P_RGY)))
        zg = jax.nn.sigmoid(part(j, P_GB)) * jax.nn.silu(part(j, P_Z))
        merged_ref[:, lo:lo + LANES] = (gated_a + _rms_gate(o, nw_ref[...], zg)).astype(BF16)


def _mixer_sample(main, ab, h0, rc0, s_all, gc0, lp, *, layer, s_acc, bb):
    bsz = main.shape[0]
    cs = functools.partial(_layer_spec, layer)
    row2 = lambda i: (i, 0)
    s_spec = pl.BlockSpec((None, bb, GDN_HEADS, GDN_DK, GDN_DV), lambda i: (layer, i, 0, 0, 0))
    has_acc = s_acc is not None
    kern = functools.partial(_mixer_sample_kernel, bb=bb, has_acc=has_acc)
    qkv3 = 3 * 3 * GDN_QK
    in_specs = [pl.BlockSpec((bb, D_MAIN), row2),
                pl.BlockSpec((bb, LANES), row2),
                pl.BlockSpec((bb, D_RNN), row2),
                pl.BlockSpec((bb, 3 * D_RNN), row2),
                s_spec,
                pl.BlockSpec((bb, qkv3), row2),
                cs((CONV_W, CONV_CH)), cs((1, D_RNN)),
                cs((N_GROUPS, LANES, 2 * LANES)),
                cs((1, D_RNN)), cs((1, D_RNN)), cs((N_GROUPS, LANES)),
                cs((1, LANES)), cs((1, LANES)), cs((1, GDN_DV))]
    args = [main, ab, h0, rc0, s_all, gc0, lp["cw"], lp["cb"], lp["wg"], lp["br"], lp["bi"], lp["lam"],
            lp["alog"], lp["dtb"], lp["nw"]]
    aliases = {}
    if has_acc:
        in_specs.append(pl.BlockSpec(memory_space=pl.ANY))
        args.append(s_acc)
        aliases = {len(args) - 1: 3}
    return pl.pallas_call(
        kern,
        grid=(bsz // bb,),
        in_specs=in_specs,
        out_specs=[pl.BlockSpec((bb, D_MODEL), row2),
                   pl.BlockSpec((bb, D_RNN), row2),
                   pl.BlockSpec((bb, 3 * D_RNN), row2),
                   s_spec,
                   pl.BlockSpec((bb, qkv3), row2)],
        out_shape=[jax.ShapeDtypeStruct((bsz, D_MODEL), BF16),
                   jax.ShapeDtypeStruct((bsz, D_RNN), F32),
                   jax.ShapeDtypeStruct((bsz, 3 * D_RNN), F32),
                   jax.ShapeDtypeStruct(s_all.shape, F32),
                   jax.ShapeDtypeStruct((bsz, qkv3), F32)],
        input_output_aliases=aliases,
        compiler_params=pltpu.CompilerParams(dimension_semantics=("arbitrary",),
                                             vmem_limit_bytes=VMEM_LIMIT),
        name="mixer_sample",
    )(*args)


def _prep_params(p):
    (ln1_g, ln1_b, ffn1_w1, ffn1_w3, ffn1_w2, w_in, b_in, rg_conv_w, rg_conv_b, rg_wr, rg_br,
     rg_wi, rg_bi, rg_lambda, gdn_conv_w, gdn_a_log, gdn_dt_bias, gdn_norm_w, w_o, ln2_g, ln2_b,
     ffn2_w1, ffn2_w3, ffn2_w2, ln3_g, ln3_b) = p
    depth = ln1_g.shape[0]
    row = lambda v: v.reshape(depth, 1, -1)

    def reorder(w):
        a0 = 2 * D_RNN + 2 * GDN_QK + 2 * GDN_VW
        g0 = a0 + 2 * GDN_HEADS
        starts = (0, 2 * D_RNN, 3 * D_RNN, 4 * D_RNN, D_RNN, 5 * D_RNN, g0, g0 + D_MODEL)
        pieces = [w[..., s:s + D_RNN].reshape(w.shape[:-1] + (N_GROUPS, 1, LANES)) for s in starts]
        main = jnp.concatenate(pieces, axis=-2).reshape(w.shape[:-1] + (D_MAIN,))
        ab = w[..., a0:g0]
        ab = jnp.pad(ab, [(0, 0)] * (w.ndim - 1) + [(0, LANES - 2 * GDN_HEADS)])
        return main, ab

    w_main, w_ab = reorder(w_in.astype(BF16))
    b_main, b_ab = reorder(row(b_in))

    def pair_blocks(w):
        w = w.reshape(depth * N_GROUPS, 2, RG_BLOCK, RG_BLOCK)
        z = jnp.zeros((depth * N_GROUPS, RG_BLOCK, RG_BLOCK), w.dtype)
        top = jnp.concatenate([w[:, 0], z], axis=2)
        bot = jnp.concatenate([z, w[:, 1]], axis=2)
        return jnp.concatenate([top, bot], axis=1).reshape(depth, N_GROUPS, LANES, LANES)

    wg = jnp.concatenate([pair_blocks(rg_wr), pair_blocks(rg_wi)], axis=3).astype(BF16)
    cw = jnp.concatenate([rg_conv_w.reshape(depth, CONV_W, N_GROUPS, 1, LANES),
                          gdn_conv_w.reshape(depth, CONV_W, 3, N_GROUPS, LANES).transpose(0, 1, 3, 2, 4)],
                         axis=3).reshape(depth, CONV_W, CONV_CH)
    pad_h = lambda v: jnp.pad(row(v), [(0, 0), (0, 0), (0, LANES - GDN_HEADS)])
    return dict(
        ln1_g=row(ln1_g), ln1_b=row(ln1_b), ln2_g=row(ln2_g), ln2_b=row(ln2_b),
        ln3_g=row(ln3_g), ln3_b=row(ln3_b),
        f1=(ffn1_w1.astype(BF16), ffn1_w3.astype(BF16), ffn1_w2.astype(BF16)),
        f2=(ffn2_w1.astype(BF16), ffn2_w3.astype(BF16), ffn2_w2.astype(BF16)),
        w_main=w_main, b_main=b_main, w_ab=w_ab, b_ab=b_ab,
        w_o=w_o.astype(BF16),
        cw=cw, cb=row(rg_conv_b),
        wg=wg, br=row(rg_br), bi=row(rg_bi), lam=rg_lambda.reshape(depth, N_GROUPS, LANES),
        alog=pad_h(gdn_a_log), dtb=pad_h(gdn_dt_bias),
        alog_c=gdn_a_log.reshape(depth, GDN_HEADS, 1), dtb_c=gdn_dt_bias.reshape(depth, GDN_HEADS, 1),
        nw=row(gdn_norm_w))


def _trunk_prompt(x, lp, *, depth, alpha, tm, tt, chunk):
    bsz, seq, _ = x.shape
    h0 = jnp.zeros((bsz, 1, D_RNN), x.dtype)
    rc0 = jnp.zeros((bsz, CONV_W - 1, D_RNN), x.dtype)
    s0 = jnp.zeros((bsz, GDN_HEADS, GDN_DK, GDN_DV), x.dtype)
    gc0 = jnp.zeros((bsz, CONV_W - 1, 3 * GDN_QK), x.dtype)
    hs, rcs, ss, gcs = [], [], [], []
    for l in range(depth):
        x1 = _ffn_ln(x.reshape(bsz * seq, D_MODEL), *lp["f1"], lp["ln1_g"], lp["ln1_b"],
                     layer=l, alpha=alpha, tm=tm)
        x2, h, rc, s, gc = _layer_prompt(x1.reshape(bsz, seq, D_MODEL), h0, rc0, s0, gc0, lp,
                                         layer=l, alpha=alpha, tt=min(tt, seq), chunk=chunk)
        x = _ffn_ln(x2.reshape(bsz * seq, D_MODEL), *lp["f2"], lp["ln3_g"], lp["ln3_b"],
                    layer=l, alpha=alpha, tm=tm).reshape(bsz, seq, D_MODEL)
        hs.append(h.reshape(bsz, D_RNN))
        rcs.append(rc)
        ss.append(s)
        gcs.append(gc)
    return x, jnp.stack(hs), jnp.stack(rcs), jnp.stack(ss), jnp.stack(gcs)


def _trunk_sample(x, h0, rc0, s0, gc0, lp, *, depth, alpha, bb):
    bsz = x.shape[0]
    xf = x.reshape(bsz, D_MODEL)
    hs, rcs, gcs = [], [], []
    s_acc = None
    for l in range(depth):
        xf = _ffn_ln(xf, *lp["f1"], lp["ln1_g"], lp["ln1_b"], layer=l, alpha=alpha, tm=bsz)
        main, ab = _in_proj(xf, lp["w_main"], lp["b_main"], lp["w_ab"], lp["b_ab"], layer=l, tm=bsz)
        merged, h, rc, s_acc, gc = _mixer_sample(
            main, ab, h0[l], rc0[l].reshape(bsz, -1), s0, gc0[l].reshape(bsz, -1), lp,
            layer=l, s_acc=s_acc, bb=bb)
        xf = _out_proj_ln(xf, merged, lp["w_o"], lp["ln2_g"], lp["ln2_b"], layer=l, alpha=alpha, tm=bsz)
        xf = _ffn_ln(xf, *lp["f2"], lp["ln3_g"], lp["ln3_b"], layer=l, alpha=alpha, tm=bsz)
        hs.append(h)
        rcs.append(rc.reshape(bsz, CONV_W - 1, D_RNN))
        gcs.append(gc.reshape(bsz, CONV_W - 1, 3 * GDN_QK))
    return xf.reshape(bsz, 1, D_MODEL), jnp.stack(hs), jnp.stack(rcs), s_acc, jnp.stack(gcs)


def kernel(x_prompt, x_sample, state_rglru_h, state_rglru_conv, state_gdn_S, state_gdn_conv, ln1_g, ln1_b, ffn1_w1, ffn1_w3, ffn1_w2, w_in, b_in, rg_conv_w, rg_conv_b, rg_wr, rg_br, rg_wi, rg_bi, rg_lambda, gdn_conv_w, gdn_a_log, gdn_dt_bias, gdn_norm_w, w_o, ln2_g, ln2_b, ffn2_w1, ffn2_w3, ffn2_w2, ln3_g, ln3_b):
    params = (ln1_g, ln1_b, ffn1_w1, ffn1_w3, ffn1_w2, w_in, b_in, rg_conv_w, rg_conv_b, rg_wr, rg_br,
              rg_wi, rg_bi, rg_lambda, gdn_conv_w, gdn_a_log, gdn_dt_bias, gdn_norm_w, w_o, ln2_g, ln2_b,
              ffn2_w1, ffn2_w3, ffn2_w2, ln3_g, ln3_b)
    depth = ln1_g.shape[0]
    lp = _prep_params(params)
    y_p, p_h, p_rc, p_s, p_gc = _trunk_prompt(x_prompt, lp, depth=depth, alpha=ALPHA, tm=512, tt=256,
                                              chunk=128)
    y_s, s_h, s_rc, s_s, s_gc = _trunk_sample(
        x_sample, state_rglru_h, state_rglru_conv, state_gdn_S, state_gdn_conv, lp,
        depth=depth, alpha=ALPHA, bb=8)
    return (y_p, y_s, p_h, p_rc, p_s, p_gc, s_h, s_rc, s_s, s_gc)
```

```python
import functools

import jax
import jax.numpy as jnp
from jax import lax
from jax.experimental import pallas as pl
from jax.experimental.pallas import tpu as pltpu

F32 = jnp.float32
BF16 = jnp.bfloat16

D_MODEL = 1024
DEPTH = 4
ALPHA = (2 * DEPTH) ** 0.25
D_RNN = D_MODEL
RG_BLOCK = 64
RG_C = 8.0
CONV_W = 4
GDN_HEADS = 8
GDN_DK = 128
GDN_DV = 128
GDN_QK = GDN_HEADS * GDN_DK
GDN_VW = GDN_HEADS * GDN_DV
D_FF = 2816
LN_EPS = 1e-5
RMS_EPS = 1e-6
LANES = 128
SUBLANES = 8
N_GROUPS = D_RNN // LANES

F_RGX, F_RGY, F_Q, F_K, F_V, F_Z = (i * D_RNN for i in range(6))
D_FRONT = 6 * D_RNN
W_IN_AB = D_FRONT
W_IN_GATES = D_FRONT + 2 * GDN_HEADS
G_GA, G_GB = 0, D_MODEL
D_GATE = 2 * D_MODEL
D_MAIN = D_FRONT + D_GATE
P_RGX, P_Q, P_K, P_V = (i * LANES for i in range(4))
CONV_GW = 4 * LANES
CONV_CH = N_GROUPS * CONV_GW
TAIL = SUBLANES
HIST = TAIL - (CONV_W - 1)
INV_BASE = 16
FF_CHUNK = 256

VMEM_LIMIT = 56 * 1024 * 1024


def _mm(a, b):
    return jnp.dot(a.astype(BF16), b.astype(BF16), preferred_element_type=F32)


def _softplus(x):
    return jnp.maximum(x, 0.0) + jnp.log1p(jnp.exp(-jnp.abs(x)))


def _layer_norm(r, g, b):
    mu = jnp.mean(r, axis=-1, keepdims=True)
    c = r - mu
    var = jnp.mean(c * c, axis=-1, keepdims=True)
    return c * lax.rsqrt(var + LN_EPS) * g + b


def _ffn_ln_math(x, w1_ref, w3_ref, w2_ref, g, b, *, alpha, ff_chunk):
    xb = x.astype(BF16)
    acc = None
    pending = None
    for c in range(D_FF // ff_chunk):
        sl = slice(c * ff_chunk, (c + 1) * ff_chunk)
        h1 = jnp.dot(xb, w1_ref[:, sl], preferred_element_type=F32)
        h3 = jnp.dot(xb, w3_ref[:, sl], preferred_element_type=F32)
        if pending is not None:
            y = jnp.dot(pending[0], w2_ref[pending[1], :], preferred_element_type=F32)
            acc = y if acc is None else acc + y
        pending = ((jax.nn.silu(h1) * h3).astype(BF16), sl)
    y = jnp.dot(pending[0], w2_ref[pending[1], :], preferred_element_type=F32)
    acc = y if acc is None else acc + y
    return _layer_norm(alpha * x + 0.5 * acc, g, b)


def _ffn_ln_kernel(x_ref, w1_ref, w3_ref, w2_ref, g_ref, b_ref, o_ref, *, alpha, ff_chunk):
    o_ref[...] = _ffn_ln_math(x_ref[...], w1_ref, w3_ref, w2_ref, g_ref[...], b_ref[...],
                              alpha=alpha, ff_chunk=ff_chunk)


def _layer_spec(layer, shape):
    nd = len(shape)
    return pl.BlockSpec((None,) + tuple(shape), lambda *_: (layer,) + (0,) * nd,
                        pipeline_mode=pl.Buffered(1))


def _ffn_ln(x, w1, w3, w2, g, b, *, layer, alpha, tm):
    m = x.shape[0]
    cs = functools.partial(_layer_spec, layer)
    return pl.pallas_call(
        functools.partial(_ffn_ln_kernel, alpha=alpha, ff_chunk=FF_CHUNK),
        grid=(m // tm,),
        in_specs=[pl.BlockSpec((tm, D_MODEL), lambda i: (i, 0)),
                  cs((D_MODEL, D_FF)), cs((D_MODEL, D_FF)),
                  cs((D_FF, D_MODEL)),
                  cs((1, D_MODEL)), cs((1, D_MODEL))],
        out_specs=pl.BlockSpec((tm, D_MODEL), lambda i: (i, 0)),
        out_shape=jax.ShapeDtypeStruct((m, D_MODEL), F32),
        compiler_params=pltpu.CompilerParams(dimension_semantics=("arbitrary",),
                                             vmem_limit_bytes=VMEM_LIMIT),
        name="ffn_ln",
    )(x, w1, w3, w2, g, b)


def _in_proj_kernel(x_ref, wf_ref, bf_ref, wgt_ref, bgt_ref, wab_ref, bab_ref, main_ref, ab_ref,
                    *, n_chunk):
    xb = x_ref[...].astype(BF16)
    for c0 in range(0, D_FRONT, n_chunk):
        main_ref[:, c0:c0 + n_chunk] = (
            jnp.dot(xb, wf_ref[:, c0:c0 + n_chunk], preferred_element_type=F32) + bf_ref[:, c0:c0 + n_chunk])
    for c0 in range(0, D_GATE, n_chunk):
        main_ref[:, D_FRONT + c0:D_FRONT + c0 + n_chunk] = (
            jnp.dot(xb, wgt_ref[:, c0:c0 + n_chunk], preferred_element_type=F32) + bgt_ref[:, c0:c0 + n_chunk])
    ab_ref[...] = jnp.dot(xb, wab_ref[...], preferred_element_type=F32) + bab_ref[...]


def _in_proj(x, w_in_b, b_front, w_gate, b_gate, w_ab, b_ab, *, layer, tm):
    m = x.shape[0]
    cs = functools.partial(_layer_spec, layer)
    return pl.pallas_call(
        functools.partial(_in_proj_kernel, n_chunk=D_RNN),
        grid=(m // tm,),
        in_specs=[pl.BlockSpec((tm, D_MODEL), lambda i: (i, 0)),
                  cs((D_MODEL, D_FRONT)), cs((1, D_FRONT)),
                  cs((D_MODEL, D_GATE)), cs((1, D_GATE)),
                  cs((D_MODEL, LANES)), cs((1, LANES))],
        out_specs=[pl.BlockSpec((tm, D_MAIN), lambda i: (i, 0)),
                   pl.BlockSpec((tm, LANES), lambda i: (i, 0))],
        out_shape=[jax.ShapeDtypeStruct((m, D_MAIN), F32),
                   jax.ShapeDtypeStruct((m, LANES), F32)],
        compiler_params=pltpu.CompilerParams(dimension_semantics=("arbitrary",),
                                             vmem_limit_bytes=VMEM_LIMIT),
        name="in_proj",
    )(x, w_in_b, b_front, w_gate, b_gate, w_ab, b_ab)


def _out_proj_ln_kernel(x_ref, m_ref, w_ref, g_ref, b_ref, o_ref, *, alpha):
    y = jnp.dot(m_ref[...], w_ref[...], preferred_element_type=F32)
    o_ref[...] = _layer_norm(alpha * x_ref[...] + y, g_ref[...], b_ref[...])


def _out_proj_ln(x, merged, w_o, g, b, *, layer, alpha, tm):
    m = x.shape[0]
    cs = functools.partial(_layer_spec, layer)
    return pl.pallas_call(
        functools.partial(_out_proj_ln_kernel, alpha=alpha),
        grid=(m // tm,),
        in_specs=[pl.BlockSpec((tm, D_MODEL), lambda i: (i, 0)),
                  pl.BlockSpec((tm, D_MODEL), lambda i: (i, 0)),
                  cs((D_MODEL, D_MODEL)),
                  cs((1, D_MODEL)), cs((1, D_MODEL))],
        out_specs=pl.BlockSpec((tm, D_MODEL), lambda i: (i, 0)),
        out_shape=jax.ShapeDtypeStruct((m, D_MODEL), F32),
        compiler_params=pltpu.CompilerParams(dimension_semantics=("arbitrary",),
                                             vmem_limit_bytes=VMEM_LIMIT),
        name="out_proj_ln",
    )(x, merged, w_o, g, b)


def _rg_gates(xc, wg, br, bi, c_lam):
    pre = _mm(xc, wg)
    r = jax.nn.sigmoid(pre[:, :LANES] + br)
    i = jax.nn.sigmoid(pre[:, LANES:] + bi)
    log_a = c_lam * r
    a = jnp.exp(log_a)
    bt = jnp.sqrt(1.0 - a * a) * (i * xc)
    return a, bt


def _l2norm(t):
    return t * lax.rsqrt(jnp.sum(t * t, axis=-1, keepdims=True) + RMS_EPS)


def _rms_gate(o, nw, zg):
    return o * lax.rsqrt(jnp.mean(o * o, axis=-1, keepdims=True) + RMS_EPS) * nw * zg


def _gdn_gates(ab, alog, dtb):
    g = -jnp.exp(alog) * _softplus(ab + dtb)
    beta = jax.nn.sigmoid(ab)
    return g, beta


def _rg_scan(a, b, h_prev):
    n = a.shape[0]
    nv = n // SUBLANES
    a3 = a.reshape(nv, SUBLANES, LANES)
    b3 = b.reshape(nv, SUBLANES, LANES)
    sub = lax.broadcasted_iota(jnp.int32, (nv, SUBLANES, LANES), 1)
    d = 1
    while d < SUBLANES:
        keep = sub >= d
        a_sh = jnp.where(keep, pltpu.roll(a3, d, axis=1), 1.0)
        b_sh = jnp.where(keep, pltpu.roll(b3, d, axis=1), 0.0)
        b3 = a3 * b_sh + b3
        a3 = a3 * a_sh
        d *= 2
    carry = h_prev
    hs = []
    for v in range(nv):
        hv = a3[v] * carry + b3[v]
        hs.append(hv)
        carry = hv[SUBLANES - 1:SUBLANES, :]
    return jnp.concatenate(hs, axis=0), carry


def _tri_inverse_many(lows, c):
    row = lax.broadcasted_iota(jnp.int32, (c, c), 0)
    col = lax.broadcasted_iota(jnp.int32, (c, c), 1)
    eye = (row == col).astype(F32)
    same = (row // INV_BASE) == (col // INV_BASE)

    def left_products(acc, pw, exp, limit):
        while exp < limit:
            if 2 * exp < limit:
                rhs = [jnp.concatenate([p, a.astype(BF16)], axis=1) for p, a in zip(pw, acc)]
                out = [jnp.dot(p, r, preferred_element_type=F32) for p, r in zip(pw, rhs)]
                pw = [o[:, :c].astype(BF16) for o in out]
                acc = [a + o[:, c:] for a, o in zip(acc, out)]
            else:
                acc = [a + jnp.dot(p, a.astype(BF16), preferred_element_type=F32) for p, a in zip(pw, acc)]
            exp *= 2
        return acc

    nd = [(-jnp.where(same, low, 0.0)) for low in lows]
    ndb = [x.astype(BF16) for x in nd]
    xs = [eye + x for x in nd]
    if INV_BASE > 2:
        sq = [jnp.dot(x, x, preferred_element_type=F32).astype(BF16) for x in ndb]
        xs = left_products(xs, sq, 2, INV_BASE)
    n_blocks = c // INV_BASE
    if n_blocks == 1:
        return xs
    xbs = [x.astype(BF16) for x in xs]
    ms = [jnp.dot(xb, jnp.where(same, 0.0, low).astype(BF16), preferred_element_type=F32).astype(BF16)
          for xb, low in zip(xbs, lows)]
    if n_blocks == 2:
        return [x - jnp.dot(m, xb, preferred_element_type=F32) for x, m, xb in zip(xs, ms, xbs)]
    out = [jnp.dot(m, jnp.concatenate([xb, m], axis=1), preferred_element_type=F32)
           for m, xb in zip(ms, xbs)]
    zs = [x - o[:, :c] for x, o in zip(xs, out)]
    m2 = [o[:, c:].astype(BF16) for o in out]
    return left_products(zs, m2, 2, n_blocks)


def _layer_prompt_kernel(x_ref, h0_ref, rc0_ref, s0_ref, gc0_ref,
                         wm_ref, bm_ref, wgt_ref, bgt_ref, wab_ref, bab_ref,
                         cw_ref, cb_ref, wg_ref, br_ref, bi_ref, lam_ref,
                         alog_ref, dtb_ref, nw_ref, wo_ref, g2_ref, b2_ref,
                         out_ref, h_ref, rc_ref, s_ref, gc_ref,
                         cbuf, h_sc, s_sc, *, tt, chunk, alpha, heads_per_pass):
    t = pl.program_id(1)
    nt = pl.num_programs(1)
    nc = tt // chunk

    @pl.when(t == 0)
    def _():
        cbuf[0:TAIL, :] = jnp.zeros((TAIL, CONV_CH), F32)
        for j in range(N_GROUPS):
            lo = j * LANES
            cbuf[HIST:TAIL, j * CONV_GW + P_RGX:j * CONV_GW + P_RGX + LANES] = rc0_ref[0, :, lo:lo + LANES]
            for c, p in enumerate((P_Q, P_K, P_V)):
                cbuf[HIST:TAIL, j * CONV_GW + p:j * CONV_GW + p + LANES] = (
                    gc0_ref[0, :, c * GDN_QK + lo:c * GDN_QK + lo + LANES])
        h_sc[...] = h0_ref[0]
        s_sc[...] = s0_ref[0]

    x = x_ref[0]
    xb = x.astype(BF16)
    ab_t = (jnp.dot(xb, wab_ref[...], preferred_element_type=F32) + bab_ref[...]).T
    g_t = -jnp.exp(alog_ref[...]) * _softplus(ab_t[0:GDN_HEADS] + dtb_ref[...])
    beta_t = jax.nn.sigmoid(ab_t[GDN_HEADS:2 * GDN_HEADS])
    lin = lax.broadcasted_iota(jnp.int32, (GDN_HEADS, tt), 1) % chunk
    d = 1
    while d < chunk:
        g_t = g_t + jnp.where(lin >= d, pltpu.roll(g_t, d, axis=1), 0.0)
        d *= 2
    cols = jnp.concatenate([g_t, beta_t, jnp.exp(g_t),
                            jnp.zeros((LANES - 3 * GDN_HEADS, tt), F32)], axis=0).T
    L_G, L_BETA, L_EG = 0, GDN_HEADS, 2 * GDN_HEADS
    c_lam_all = -RG_C * _softplus(-lam_ref[...])

    rw = lax.broadcasted_iota(jnp.int32, (chunk, chunk), 0)
    cl = lax.broadcasted_iota(jnp.int32, (chunk, chunk), 1)
    causal = rw >= cl
    strict = rw > cl
    sl = lambda arr, n: arr[n * chunk:(n + 1) * chunk]
    merged = []
    for j0 in range(0, N_GROUPS, heads_per_pass):
        heads = range(j0, j0 + heads_per_pass)

        pw = heads_per_pass * LANES

        def project(w_ref, b_ref, off):
            return (jnp.dot(xb, w_ref[:, off + j0 * LANES:off + j0 * LANES + pw], preferred_element_type=F32)
                    + b_ref[:, off + j0 * LANES:off + j0 * LANES + pw])

        pieces = {P_RGX: project(wm_ref, bm_ref, F_RGX), P_Q: project(wm_ref, bm_ref, F_Q),
                  P_K: project(wm_ref, bm_ref, F_K), P_V: project(wm_ref, bm_ref, F_V)}
        r_rgy = project(wm_ref, bm_ref, F_RGY)
        r_z = project(wm_ref, bm_ref, F_Z)
        r_ga = project(wgt_ref, bgt_ref, G_GA)
        r_gb = project(wgt_ref, bgt_ref, G_GB)
        gated_a, zgs, qs, ks, vs = {}, {}, {}, {}, {}
        for j in heads:
            lo = j * LANES
            cg = j * CONV_GW
            hl = slice((j - j0) * LANES, (j - j0 + 1) * LANES)
            for p, piece in pieces.items():
                cbuf[TAIL:TAIL + tt, cg + p:cg + p + LANES] = piece[:, hl]

            def conv(p):
                acc = cw_ref[CONV_W - 1:CONV_W, cg + p:cg + p + LANES] * pieces[p][:, hl]
                for i in range(CONV_W - 1):
                    acc = acc + (cw_ref[i:i + 1, cg + p:cg + p + LANES]
                                 * cbuf[HIST + i:HIST + i + tt, cg + p:cg + p + LANES])
                return acc

            xc = conv(P_RGX) + cb_ref[:, lo:lo + LANES]
            a, bt = _rg_gates(xc, wg_ref[j], br_ref[:, lo:lo + LANES], bi_ref[:, lo:lo + LANES],
                              c_lam_all[j:j + 1, :])
            hseq, h_last = _rg_scan(a, bt, h_sc[:, lo:lo + LANES])
            h_sc[:, lo:lo + LANES] = h_last
            gated_a[j] = jax.nn.sigmoid(r_ga[:, hl]) * (hseq * jax.nn.gelu(r_rgy[:, hl]))
            zgs[j] = jax.nn.sigmoid(r_gb[:, hl]) * jax.nn.silu(r_z[:, hl])
            qs[j] = _l2norm(jax.nn.silu(conv(P_Q))) * (GDN_DK ** -0.5)
            ks[j] = _l2norm(jax.nn.silu(conv(P_K)))
            vs[j] = jax.nn.silu(conv(P_V))

        c0, c1 = j0 * CONV_GW, (j0 + heads_per_pass) * CONV_GW
        cbuf[0:TAIL, c0:c1] = cbuf[tt:tt + TAIL, c0:c1]

        probs = [(j, n) for j in heads for n in range(nc)]
        gcol = {p: sl(cols, p[1])[:, L_G + p[0]:L_G + p[0] + 1] for p in probs}
        egc = {p: sl(cols, p[1])[:, L_EG + p[0]:L_EG + p[0] + 1] for p in probs}
        bcol = {p: sl(cols, p[1])[:, L_BETA + p[0]:L_BETA + p[0] + 1] for p in probs}
        decay = {p: jnp.exp(jnp.where(causal, gcol[p] - g_t[p[0]:p[0] + 1, p[1] * chunk:(p[1] + 1) * chunk],
                                      0.0)) for p in probs}
        kn = {p: sl(ks[p[0]], p[1]) for p in probs}
        qn = {p: sl(qs[p[0]], p[1]) for p in probs}
        kb = {p: kn[p] * bcol[p] for p in probs}
        knb = {p: kn[p].astype(BF16) for p in probs}
        kk = {p: lax.dot_general(kb[p].astype(BF16), knb[p], (((1,), (1,)), ((), ())),
                                 preferred_element_type=F32) for p in probs}
        qk = {p: lax.dot_general(qn[p].astype(BF16), knb[p], (((1,), (1,)), ((), ())),
                                 preferred_element_type=F32) for p in probs}
        lows = [jnp.where(strict, kk[p] * decay[p], 0.0) for p in probs]
        a_intra = {p: jnp.where(causal, qk[p] * decay[p], 0.0).astype(BF16) for p in probs}
        tinv = dict(zip(probs, _tri_inverse_many(lows, chunk)))
        sol = {p: jnp.dot(tinv[p].astype(BF16),
                          jnp.concatenate([sl(vs[p[0]], p[1]) * bcol[p], kb[p] * egc[p]], axis=1).astype(BF16),
                          preferred_element_type=F32) for p in probs}
        g_last = {p: gcol[p][chunk - 1:chunk, :] for p in probs}
        wq = {p: jnp.concatenate([sol[p][:, GDN_DV:], qn[p] * egc[p]], axis=0).astype(BF16) for p in probs}
        k_tail = {p: (kn[p] * jnp.exp(g_last[p] - gcol[p])).astype(BF16) for p in probs}

        states = {j: s_sc[j] for j in heads}
        outs = {}
        for n in range(nc):
            ws = {j: jnp.dot(wq[(j, n)], states[j].astype(BF16), preferred_element_type=F32) for j in heads}
            v_new = {j: (sol[(j, n)][:, :GDN_DV] - ws[j][:chunk]).astype(BF16) for j in heads}
            for j in heads:
                outs[(j, n)] = ws[j][chunk:] + jnp.dot(a_intra[(j, n)], v_new[j], preferred_element_type=F32)
            states = {j: states[j] * jnp.exp(g_last[(j, n)])
                      + lax.dot_general(k_tail[(j, n)], v_new[j], (((0,), (0,)), ((), ())),
                                        preferred_element_type=F32) for j in heads}
        for j in heads:
            s_sc[j] = states[j]

        for j in heads:
            o = outs[(j, 0)] if nc == 1 else jnp.concatenate([outs[(j, n)] for n in range(nc)], axis=0)
            merged.append((gated_a[j] + _rms_gate(o, nw_ref[...], zgs[j])).astype(BF16))

    m = jnp.concatenate(merged, axis=1)
    y = jnp.dot(m, wo_ref[...], preferred_element_type=F32)
    out_ref[0] = _layer_norm(alpha * x + y, g2_ref[...], b2_ref[...])

    @pl.when(t == nt - 1)
    def _():
        h_ref[0] = h_sc[...]
        for j in range(N_GROUPS):
            lo = j * LANES
            rc_ref[0, :, lo:lo + LANES] = cbuf[HIST:TAIL, j * CONV_GW + P_RGX:j * CONV_GW + P_RGX + LANES]
            for c, p in enumerate((P_Q, P_K, P_V)):
                gc_ref[0, :, c * GDN_QK + lo:c * GDN_QK + lo + LANES] = (
                    cbuf[HIST:TAIL, j * CONV_GW + p:j * CONV_GW + p + LANES])
        s_ref[0] = s_sc[...]


def _layer_prompt(x, h0, rc0, s0, gc0, lp, *, layer, alpha, tt, chunk):
    bsz, seq, _ = x.shape
    nt = seq // tt
    cs = functools.partial(_layer_spec, layer)
    per_b3 = lambda b, t: (b, 0, 0)
    per_b4 = lambda b, t: (b, 0, 0, 0)
    kern = functools.partial(_layer_prompt_kernel, tt=tt, chunk=chunk, alpha=alpha,
                             heads_per_pass=N_GROUPS // 2)
    return pl.pallas_call(
        kern,
        grid=(bsz, nt),
        in_specs=[pl.BlockSpec((1, tt, D_MODEL), lambda b, t: (b, t, 0)),
                  pl.BlockSpec((1, 1, D_RNN), per_b3),
                  pl.BlockSpec((1, CONV_W - 1, D_RNN), per_b3),
                  pl.BlockSpec((1, GDN_HEADS, GDN_DK, GDN_DV), per_b4),
                  pl.BlockSpec((1, CONV_W - 1, 3 * GDN_QK), per_b3),
                  cs((D_MODEL, D_FRONT)), cs((1, D_FRONT)), cs((D_MODEL, D_GATE)), cs((1, D_GATE)),
                  cs((D_MODEL, LANES)), cs((1, LANES)),
                  cs((CONV_W, CONV_CH)), cs((1, D_RNN)),
                  cs((N_GROUPS, LANES, 2 * LANES)),
                  cs((1, D_RNN)), cs((1, D_RNN)), cs((N_GROUPS, LANES)),
                  cs((GDN_HEADS, 1)), cs((GDN_HEADS, 1)), cs((1, GDN_DV)),
                  cs((D_MODEL, D_MODEL)), cs((1, D_MODEL)), cs((1, D_MODEL))],
        out_specs=[pl.BlockSpec((1, tt, D_MODEL), lambda b, t: (b, t, 0)),
                   pl.BlockSpec((1, 1, D_RNN), per_b3),
                   pl.BlockSpec((1, CONV_W - 1, D_RNN), per_b3),
                   pl.BlockSpec((1, GDN_HEADS, GDN_DK, GDN_DV), per_b4),
                   pl.BlockSpec((1, CONV_W - 1, 3 * GDN_QK), per_b3)],
        out_shape=[jax.ShapeDtypeStruct((bsz, seq, D_MODEL), F32),
                   jax.ShapeDtypeStruct((bsz, 1, D_RNN), F32),
                   jax.ShapeDtypeStruct((bsz, CONV_W - 1, D_RNN), F32),
                   jax.ShapeDtypeStruct((bsz, GDN_HEADS, GDN_DK, GDN_DV), F32),
                   jax.ShapeDtypeStruct((bsz, CONV_W - 1, 3 * GDN_QK), F32)],
        scratch_shapes=[pltpu.VMEM((tt + TAIL, CONV_CH), F32),
                        pltpu.VMEM((1, D_RNN), F32),
                        pltpu.VMEM((GDN_HEADS, GDN_DK, GDN_DV), F32)],
        compiler_params=pltpu.CompilerParams(dimension_semantics=("arbitrary", "arbitrary"),
                                             vmem_limit_bytes=VMEM_LIMIT),
        name="mixer_prompt",
    )(x, h0, rc0, s0, gc0,
      lp["w_in_b"], lp["b_front"], lp["w_gate"], lp["b_gate"], lp["w_ab"], lp["b_ab"],
      lp["cw"], lp["cb"], lp["wg"], lp["br"], lp["bi"], lp["lam"],
      lp["alog_c"], lp["dtb_c"], lp["nw"], lp["w_o"], lp["ln2_g"], lp["ln2_b"])


def _mixer_sample_kernel(main_ref, ab_ref, h0_ref, rc0_ref, s0_ref, gc0_ref,
                         cw_ref, cb_ref, wg_ref, br_ref, bi_ref, lam_ref,
                         alog_ref, dtb_ref, nw_ref, *refs, bb, has_acc):
    merged_ref, h_ref, rc_ref, s_ref, gc_ref = refs[1:] if has_acc else refs
    nprev = CONV_W - 1

    def part(off, j):
        return main_ref[:, off + j * LANES:off + (j + 1) * LANES]

    def conv(hist_ref, ch, off_hist, j, p, off_main):
        cc = j * CONV_GW + p
        acc = cw_ref[nprev:CONV_W, cc:cc + LANES] * part(off_main, j)
        for i in range(nprev):
            acc = acc + (cw_ref[i:i + 1, cc:cc + LANES]
                         * hist_ref[:, i * ch + off_hist:i * ch + off_hist + LANES])
        return acc

    qkv_w = 3 * GDN_QK
    rc_ref[:, 0:2 * D_RNN] = rc0_ref[:, D_RNN:3 * D_RNN]
    rc_ref[:, 2 * D_RNN:3 * D_RNN] = main_ref[:, F_RGX:F_RGX + D_RNN]
    gc_ref[:, 0:2 * qkv_w] = gc0_ref[:, qkv_w:3 * qkv_w]
    gc_ref[:, 2 * qkv_w:3 * qkv_w] = main_ref[:, F_Q:F_Q + qkv_w]

    g, beta = _gdn_gates(ab_ref[...], alog_ref[...], dtb_ref[...])
    eg = jnp.exp(g)
    c_lam_all = -RG_C * _softplus(-lam_ref[...])

    for j in range(N_GROUPS):
        lo = j * LANES
        xc = conv(rc0_ref, D_RNN, lo, j, P_RGX, F_RGX) + cb_ref[:, lo:lo + LANES]
        a, bt = _rg_gates(xc, wg_ref[j], br_ref[:, lo:lo + LANES], bi_ref[:, lo:lo + LANES],
                          c_lam_all[j:j + 1, :])
        hseq = a * h0_ref[:, lo:lo + LANES] + bt
        h_ref[:, lo:lo + LANES] = hseq

        qn = _l2norm(jax.nn.silu(conv(gc0_ref, qkv_w, lo, j, P_Q, F_Q))) * (GDN_DK ** -0.5)
        kn = _l2norm(jax.nn.silu(conv(gc0_ref, qkv_w, GDN_QK + lo, j, P_K, F_K)))
        v = jax.nn.silu(conv(gc0_ref, qkv_w, 2 * GDN_QK + lo, j, P_V, F_V))
        qn_t = qn.T
        kn_t = kn.T
        o_rows = []
        for b in range(bb):
            s = s0_ref[b, j]
            kcol = kn_t[:, b:b + 1]
            egb = eg[b:b + 1, j:j + 1]
            ks = jnp.sum(s * kcol, axis=0, keepdims=True)
            v_new = beta[b:b + 1, GDN_HEADS + j:GDN_HEADS + j + 1] * (v[b:b + 1, :] - egb * ks)
            s_new = s * egb + kcol * v_new
            s_ref[b, j] = s_new
            o_rows.append(jnp.sum(s_new * qn_t[:, b:b + 1], axis=0, keepdims=True))
        o = jnp.concatenate(o_rows, axis=0)
        gated_a = jax.nn.sigmoid(part(D_FRONT + G_GA, j)) * (hseq * jax.nn.gelu(part(F_RGY, j)))
        zg = jax.nn.sigmoid(part(D_FRONT + G_GB, j)) * jax.nn.silu(part(F_Z, j))
        merged_ref[:, lo:lo + LANES] = (gated_a + _rms_gate(o, nw_ref[...], zg)).astype(BF16)


def _mixer_sample(main, ab, h0, rc0, s_all, gc0, lp, *, layer, s_acc, bb):
    bsz = main.shape[0]
    cs = functools.partial(_layer_spec, layer)
    row2 = lambda i: (i, 0)
    s_spec = pl.BlockSpec((None, bb, GDN_HEADS, GDN_DK, GDN_DV), lambda i: (layer, i, 0, 0, 0))
    has_acc = s_acc is not None
    kern = functools.partial(_mixer_sample_kernel, bb=bb, has_acc=has_acc)
    qkv3 = 3 * 3 * GDN_QK
    in_specs = [pl.BlockSpec((bb, D_MAIN), row2),
                pl.BlockSpec((bb, LANES), row2),
                pl.BlockSpec((bb, D_RNN), row2),
                pl.BlockSpec((bb, 3 * D_RNN), row2),
                s_spec,
                pl.BlockSpec((bb, qkv3), row2),
                cs((CONV_W, CONV_CH)), cs((1, D_RNN)),
                cs((N_GROUPS, LANES, 2 * LANES)),
                cs((1, D_RNN)), cs((1, D_RNN)), cs((N_GROUPS, LANES)),
                cs((1, LANES)), cs((1, LANES)), cs((1, GDN_DV))]
    args = [main, ab, h0, rc0, s_all, gc0, lp["cw"], lp["cb"], lp["wg"], lp["br"], lp["bi"], lp["lam"],
            lp["alog"], lp["dtb"], lp["nw"]]
    aliases = {}
    if has_acc:
        in_specs.append(pl.BlockSpec(memory_space=pl.ANY))
        args.append(s_acc)
        aliases = {len(args) - 1: 3}
    return pl.pallas_call(
        kern,
        grid=(bsz // bb,),
        in_specs=in_specs,
        out_specs=[pl.BlockSpec((bb, D_MODEL), row2),
                   pl.BlockSpec((bb, D_RNN), row2),
                   pl.BlockSpec((bb, 3 * D_RNN), row2),
                   s_spec,
                   pl.BlockSpec((bb, qkv3), row2)],
        out_shape=[jax.ShapeDtypeStruct((bsz, D_MODEL), BF16),
                   jax.ShapeDtypeStruct((bsz, D_RNN), F32),
                   jax.ShapeDtypeStruct((bsz, 3 * D_RNN), F32),
                   jax.ShapeDtypeStruct(s_all.shape, F32),
                   jax.ShapeDtypeStruct((bsz, qkv3), F32)],
        input_output_aliases=aliases,
        compiler_params=pltpu.CompilerParams(dimension_semantics=("arbitrary",),
                                             vmem_limit_bytes=VMEM_LIMIT),
        name="mixer_sample",
    )(*args)


def _prep_params(p):
    (ln1_g, ln1_b, ffn1_w1, ffn1_w3, ffn1_w2, w_in, b_in, rg_conv_w, rg_conv_b, rg_wr, rg_br,
     rg_wi, rg_bi, rg_lambda, gdn_conv_w, gdn_a_log, gdn_dt_bias, gdn_norm_w, w_o, ln2_g, ln2_b,
     ffn2_w1, ffn2_w3, ffn2_w2, ln3_g, ln3_b) = p
    depth = ln1_g.shape[0]
    row = lambda v: v.reshape(depth, 1, -1)

    def split(w):
        ab = jnp.pad(w[..., W_IN_AB:W_IN_GATES], [(0, 0)] * (w.ndim - 1) + [(0, LANES - 2 * GDN_HEADS)])
        return ab, w[..., W_IN_GATES:]

    w_in_b = w_in.astype(BF16)
    w_ab, w_gate = split(w_in_b)
    b_ab, b_gate = split(row(b_in))

    def pair_blocks(w):
        w = w.reshape(depth * N_GROUPS, 2, RG_BLOCK, RG_BLOCK)
        z = jnp.zeros((depth * N_GROUPS, RG_BLOCK, RG_BLOCK), w.dtype)
        top = jnp.concatenate([w[:, 0], z], axis=2)
        bot = jnp.concatenate([z, w[:, 1]], axis=2)
        return jnp.concatenate([top, bot], axis=1).reshape(depth, N_GROUPS, LANES, LANES)

    wg = jnp.concatenate([pair_blocks(rg_wr), pair_blocks(rg_wi)], axis=3).astype(BF16)
    cw = jnp.concatenate([rg_conv_w.reshape(depth, CONV_W, N_GROUPS, 1, LANES),
                          gdn_conv_w.reshape(depth, CONV_W, 3, N_GROUPS, LANES).transpose(0, 1, 3, 2, 4)],
                         axis=3).reshape(depth, CONV_W, CONV_CH)
    pad_h = lambda v: jnp.pad(row(v), [(0, 0), (0, 0), (0, LANES - GDN_HEADS)])
    return dict(
        ln1_g=row(ln1_g), ln1_b=row(ln1_b), ln2_g=row(ln2_g), ln2_b=row(ln2_b),
        ln3_g=row(ln3_g), ln3_b=row(ln3_b),
        f1=(ffn1_w1.astype(BF16), ffn1_w3.astype(BF16), ffn1_w2.astype(BF16)),
        f2=(ffn2_w1.astype(BF16), ffn2_w3.astype(BF16), ffn2_w2.astype(BF16)),
        w_in_b=w_in_b, b_front=row(b_in)[..., :D_FRONT], w_gate=w_gate, b_gate=b_gate,
        w_ab=w_ab, b_ab=b_ab,
        w_o=w_o.astype(BF16),
        cw=cw, cb=row(rg_conv_b),
        wg=wg, br=row(rg_br), bi=row(rg_bi), lam=rg_lambda.reshape(depth, N_GROUPS, LANES),
        alog=pad_h(gdn_a_log), dtb=pad_h(gdn_dt_bias),
        alog_c=gdn_a_log.reshape(depth, GDN_HEADS, 1), dtb_c=gdn_dt_bias.reshape(depth, GDN_HEADS, 1),
        nw=row(gdn_norm_w))


def _trunk_prompt(x, lp, *, depth, alpha, tm, tt, chunk):
    bsz, seq, _ = x.shape
    h0 = jnp.zeros((bsz, 1, D_RNN), x.dtype)
    rc0 = jnp.zeros((bsz, CONV_W - 1, D_RNN), x.dtype)
    s0 = jnp.zeros((bsz, GDN_HEADS, GDN_DK, GDN_DV), x.dtype)
    gc0 = jnp.zeros((bsz, CONV_W - 1, 3 * GDN_QK), x.dtype)
    hs, rcs, ss, gcs = [], [], [], []
    for l in range(depth):
        x1 = _ffn_ln(x.reshape(bsz * seq, D_MODEL), *lp["f1"], lp["ln1_g"], lp["ln1_b"],
                     layer=l, alpha=alpha, tm=tm)
        x2, h, rc, s, gc = _layer_prompt(x1.reshape(bsz, seq, D_MODEL), h0, rc0, s0, gc0, lp,
                                         layer=l, alpha=alpha, tt=min(tt, seq), chunk=chunk)
        x = _ffn_ln(x2.reshape(bsz * seq, D_MODEL), *lp["f2"], lp["ln3_g"], lp["ln3_b"],
                    layer=l, alpha=alpha, tm=tm).reshape(bsz, seq, D_MODEL)
        hs.append(h.reshape(bsz, D_RNN))
        rcs.append(rc)
        ss.append(s)
        gcs.append(gc)
    return x, jnp.stack(hs), jnp.stack(rcs), jnp.stack(ss), jnp.stack(gcs)


def _trunk_sample(x, h0, rc0, s0, gc0, lp, *, depth, alpha, bb):
    bsz = x.shape[0]
    xf = x.reshape(bsz, D_MODEL)
    hs, rcs, gcs = [], [], []
    s_acc = None
    for l in range(depth):
        xf = _ffn_ln(xf, *lp["f1"], lp["ln1_g"], lp["ln1_b"], layer=l, alpha=alpha, tm=bsz)
        main, ab = _in_proj(xf, lp["w_in_b"], lp["b_front"], lp["w_gate"], lp["b_gate"], lp["w_ab"],
                            lp["b_ab"], layer=l, tm=bsz)
        merged, h, rc, s_acc, gc = _mixer_sample(
            main, ab, h0[l], rc0[l].reshape(bsz, -1), s0, gc0[l].reshape(bsz, -1), lp,
            layer=l, s_acc=s_acc, bb=bb)
        xf = _out_proj_ln(xf, merged, lp["w_o"], lp["ln2_g"], lp["ln2_b"], layer=l, alpha=alpha, tm=bsz)
        xf = _ffn_ln(xf, *lp["f2"], lp["ln3_g"], lp["ln3_b"], layer=l, alpha=alpha, tm=bsz)
        hs.append(h)
        rcs.append(rc.reshape(bsz, CONV_W - 1, D_RNN))
        gcs.append(gc.reshape(bsz, CONV_W - 1, 3 * GDN_QK))
    return xf.reshape(bsz, 1, D_MODEL), jnp.stack(hs), jnp.stack(rcs), s_acc, jnp.stack(gcs)


def kernel(x_prompt, x_sample, state_rglru_h, state_rglru_conv, state_gdn_S, state_gdn_conv, ln1_g, ln1_b, ffn1_w1, ffn1_w3, ffn1_w2, w_in, b_in, rg_conv_w, rg_conv_b, rg_wr, rg_br, rg_wi, rg_bi, rg_lambda, gdn_conv_w, gdn_a_log, gdn_dt_bias, gdn_norm_w, w_o, ln2_g, ln2_b, ffn2_w1, ffn2_w3, ffn2_w2, ln3_g, ln3_b):
    params = (ln1_g, ln1_b, ffn1_w1, ffn1_w3, ffn1_w2, w_in, b_in, rg_conv_w, rg_conv_b, rg_wr, rg_br,
              rg_wi, rg_bi, rg_lambda, gdn_conv_w, gdn_a_log, gdn_dt_bias, gdn_norm_w, w_o, ln2_g, ln2_b,
              ffn2_w1, ffn2_w3, ffn2_w2, ln3_g, ln3_b)
    depth = ln1_g.shape[0]
    lp = _prep_params(params)
    y_p, p_h, p_rc, p_s, p_gc = _trunk_prompt(x_prompt, lp, depth=depth, alpha=ALPHA, tm=512, tt=256,
                                              chunk=128)
    y_s, s_h, s_rc, s_s, s_gc = _trunk_sample(
        x_sample, state_rglru_h, state_rglru_conv, state_gdn_S, state_gdn_conv, lp,
        depth=depth, alpha=ALPHA, bb=8)
    return (y_p, y_s, p_h, p_rc, p_s, p_gc, s_h, s_rc, s_s, s_gc)
```

```python
import functools

import jax
import jax.numpy as jnp
from jax import lax
from jax.experimental import pallas as pl
from jax.experimental.pallas import tpu as pltpu

F32 = jnp.float32
BF16 = jnp.bfloat16

D_MODEL = 1024
DEPTH = 4
ALPHA = (2 * DEPTH) ** 0.25
D_RNN = D_MODEL
RG_BLOCK = 64
RG_C = 8.0
CONV_W = 4
GDN_HEADS = 8
GDN_DK = 128
GDN_DV = 128
GDN_QK = GDN_HEADS * GDN_DK
GDN_VW = GDN_HEADS * GDN_DV
D_FF = 2816
LN_EPS = 1e-5
RMS_EPS = 1e-6
LANES = 128
SUBLANES = 8
N_GROUPS = D_RNN // LANES

F_RGX, F_RGY, F_Q, F_K, F_V, F_Z = (i * D_RNN for i in range(6))
D_FRONT = 6 * D_RNN
W_IN_AB = D_FRONT
W_IN_GATES = D_FRONT + 2 * GDN_HEADS
G_GA, G_GB = 0, D_MODEL
D_GATE = 2 * D_MODEL
D_MAIN = D_FRONT + D_GATE
P_RGX, P_Q, P_K, P_V = (i * LANES for i in range(4))
CONV_GW = 4 * LANES
CONV_CH = N_GROUPS * CONV_GW
TAIL = SUBLANES
HIST = TAIL - (CONV_W - 1)
INV_BASE = 16
FF_CHUNK = 256

VMEM_LIMIT = 56 * 1024 * 1024


def _mm(a, b):
    return jnp.dot(a.astype(BF16), b.astype(BF16), preferred_element_type=F32)


def _softplus(x):
    return jnp.maximum(x, 0.0) + jnp.log1p(jnp.exp(-jnp.abs(x)))


def _layer_norm(r, g, b):
    mu = jnp.mean(r, axis=-1, keepdims=True)
    c = r - mu
    var = jnp.mean(c * c, axis=-1, keepdims=True)
    return c * lax.rsqrt(var + LN_EPS) * g + b


def _ffn_ln_math(x, w1_ref, w3_ref, w2_ref, g, b, *, alpha, ff_chunk):
    xb = x.astype(BF16)
    acc = None
    pending = None
    for c in range(D_FF // ff_chunk):
        sl = slice(c * ff_chunk, (c + 1) * ff_chunk)
        h1 = jnp.dot(xb, w1_ref[:, sl], preferred_element_type=F32)
        h3 = jnp.dot(xb, w3_ref[:, sl], preferred_element_type=F32)
        if pending is not None:
            y = jnp.dot(pending[0], w2_ref[pending[1], :], preferred_element_type=F32)
            acc = y if acc is None else acc + y
        pending = ((jax.nn.silu(h1) * h3).astype(BF16), sl)
    y = jnp.dot(pending[0], w2_ref[pending[1], :], preferred_element_type=F32)
    acc = y if acc is None else acc + y
    return _layer_norm(alpha * x + 0.5 * acc, g, b)


def _ffn_ln_kernel(x_ref, w1_ref, w3_ref, w2_ref, g_ref, b_ref, o_ref, *, alpha, ff_chunk):
    o_ref[...] = _ffn_ln_math(x_ref[...], w1_ref, w3_ref, w2_ref, g_ref[...], b_ref[...],
                              alpha=alpha, ff_chunk=ff_chunk)


def _layer_spec(layer, shape):
    nd = len(shape)
    return pl.BlockSpec((None,) + tuple(shape), lambda *_: (layer,) + (0,) * nd,
                        pipeline_mode=pl.Buffered(1))


def _ffn_ln(x, w1, w3, w2, g, b, *, layer, alpha, tm):
    m = x.shape[0]
    cs = functools.partial(_layer_spec, layer)
    return pl.pallas_call(
        functools.partial(_ffn_ln_kernel, alpha=alpha, ff_chunk=FF_CHUNK),
        grid=(m // tm,),
        in_specs=[pl.BlockSpec((tm, D_MODEL), lambda i: (i, 0)),
                  cs((D_MODEL, D_FF)), cs((D_MODEL, D_FF)),
                  cs((D_FF, D_MODEL)),
                  cs((1, D_MODEL)), cs((1, D_MODEL))],
        out_specs=pl.BlockSpec((tm, D_MODEL), lambda i: (i, 0)),
        out_shape=jax.ShapeDtypeStruct((m, D_MODEL), F32),
        compiler_params=pltpu.CompilerParams(dimension_semantics=("arbitrary",),
                                             vmem_limit_bytes=VMEM_LIMIT),
        name="ffn_ln",
    )(x, w1, w3, w2, g, b)


def _in_proj_kernel(x_ref, wf_ref, bf_ref, wgt_ref, bgt_ref, wab_ref, bab_ref, main_ref, ab_ref,
                    *, n_chunk):
    xb = x_ref[...].astype(BF16)
    for c0 in range(0, D_FRONT, n_chunk):
        main_ref[:, c0:c0 + n_chunk] = (
            jnp.dot(xb, wf_ref[:, c0:c0 + n_chunk], preferred_element_type=F32) + bf_ref[:, c0:c0 + n_chunk])
    for c0 in range(0, D_GATE, n_chunk):
        main_ref[:, D_FRONT + c0:D_FRONT + c0 + n_chunk] = (
            jnp.dot(xb, wgt_ref[:, c0:c0 + n_chunk], preferred_element_type=F32) + bgt_ref[:, c0:c0 + n_chunk])
    ab_ref[...] = jnp.dot(xb, wab_ref[...], preferred_element_type=F32) + bab_ref[...]


def _in_proj(x, w_in_b, b_front, w_gate, b_gate, w_ab, b_ab, *, layer, tm):
    m = x.shape[0]
    cs = functools.partial(_layer_spec, layer)
    return pl.pallas_call(
        functools.partial(_in_proj_kernel, n_chunk=D_RNN),
        grid=(m // tm,),
        in_specs=[pl.BlockSpec((tm, D_MODEL), lambda i: (i, 0)),
                  cs((D_MODEL, D_FRONT)), cs((1, D_FRONT)),
                  cs((D_MODEL, D_GATE)), cs((1, D_GATE)),
                  cs((D_MODEL, LANES)), cs((1, LANES))],
        out_specs=[pl.BlockSpec((tm, D_MAIN), lambda i: (i, 0)),
                   pl.BlockSpec((tm, LANES), lambda i: (i, 0))],
        out_shape=[jax.ShapeDtypeStruct((m, D_MAIN), F32),
                   jax.ShapeDtypeStruct((m, LANES), F32)],
        compiler_params=pltpu.CompilerParams(dimension_semantics=("arbitrary",),
                                             vmem_limit_bytes=VMEM_LIMIT),
        name="in_proj",
    )(x, w_in_b, b_front, w_gate, b_gate, w_ab, b_ab)


def _out_proj_ln_kernel(x_ref, m_ref, w_ref, g_ref, b_ref, o_ref, *, alpha):
    y = jnp.dot(m_ref[...], w_ref[...], preferred_element_type=F32)
    o_ref[...] = _layer_norm(alpha * x_ref[...] + y, g_ref[...], b_ref[...])


def _out_proj_ln(x, merged, w_o, g, b, *, layer, alpha, tm):
    m = x.shape[0]
    cs = functools.partial(_layer_spec, layer)
    return pl.pallas_call(
        functools.partial(_out_proj_ln_kernel, alpha=alpha),
        grid=(m // tm,),
        in_specs=[pl.BlockSpec((tm, D_MODEL), lambda i: (i, 0)),
                  pl.BlockSpec((tm, D_MODEL), lambda i: (i, 0)),
                  cs((D_MODEL, D_MODEL)),
                  cs((1, D_MODEL)), cs((1, D_MODEL))],
        out_specs=pl.BlockSpec((tm, D_MODEL), lambda i: (i, 0)),
        out_shape=jax.ShapeDtypeStruct((m, D_MODEL), F32),
        compiler_params=pltpu.CompilerParams(dimension_semantics=("arbitrary",),
                                             vmem_limit_bytes=VMEM_LIMIT),
        name="out_proj_ln",
    )(x, merged, w_o, g, b)


def _rg_gates(xc, wg, br, bi, c_lam):
    pre = _mm(xc, wg)
    r = jax.nn.sigmoid(pre[:, :LANES] + br)
    i = jax.nn.sigmoid(pre[:, LANES:] + bi)
    log_a = c_lam * r
    a = jnp.exp(log_a)
    bt = jnp.sqrt(1.0 - a * a) * (i * xc)
    return a, bt


def _l2norm(t):
    return t * lax.rsqrt(jnp.sum(t * t, axis=-1, keepdims=True) + RMS_EPS)


def _rms_gate(o, nw, zg):
    return o * lax.rsqrt(jnp.mean(o * o, axis=-1, keepdims=True) + RMS_EPS) * nw * zg


def _gdn_gates(ab, alog, dtb):
    g = -jnp.exp(alog) * _softplus(ab + dtb)
    beta = jax.nn.sigmoid(ab)
    return g, beta


def _rg_scan(a, b, h_prev):
    n = a.shape[0]
    nv = n // SUBLANES
    a3 = a.reshape(nv, SUBLANES, LANES)
    b3 = b.reshape(nv, SUBLANES, LANES)
    sub = lax.broadcasted_iota(jnp.int32, (nv, SUBLANES, LANES), 1)
    d = 1
    while d < SUBLANES:
        keep = sub >= d
        a_sh = jnp.where(keep, pltpu.roll(a3, d, axis=1), 1.0)
        b_sh = jnp.where(keep, pltpu.roll(b3, d, axis=1), 0.0)
        b3 = a3 * b_sh + b3
        a3 = a3 * a_sh
        d *= 2
    carry = h_prev
    hs = []
    for v in range(nv):
        hv = a3[v] * carry + b3[v]
        hs.append(hv)
        carry = hv[SUBLANES - 1:SUBLANES, :]
    return jnp.concatenate(hs, axis=0), carry


def _dot_many(xs, ys):
    return [jnp.dot(x, y, preferred_element_type=F32) for x, y in zip(xs, ys)]


def _dot_nt_many(xs, ys):
    return [lax.dot_general(x, y, (((1,), (1,)), ((), ())), preferred_element_type=F32)
            for x, y in zip(xs, ys)]


def _tri_inverse_many(lows, c):
    row = lax.broadcasted_iota(jnp.int32, (c, c), 0)
    col = lax.broadcasted_iota(jnp.int32, (c, c), 1)
    eye = (row == col).astype(F32)
    same = (row // INV_BASE) == (col // INV_BASE)

    def left_products(acc, pw, exp, limit):
        while exp < limit:
            if 2 * exp < limit:
                rhs = [jnp.concatenate([p, a.astype(BF16)], axis=1) for p, a in zip(pw, acc)]
                out = [jnp.dot(p, r, preferred_element_type=F32) for p, r in zip(pw, rhs)]
                pw = [o[:, :c].astype(BF16) for o in out]
                acc = [a + o[:, c:] for a, o in zip(acc, out)]
            else:
                acc = [a + o for a, o in zip(acc, _dot_many(pw, [a.astype(BF16) for a in acc]))]
            exp *= 2
        return acc

    nd = [(-jnp.where(same, low, 0.0)) for low in lows]
    ndb = [x.astype(BF16) for x in nd]
    xs = [eye + x for x in nd]
    if INV_BASE > 2:
        sq = [o.astype(BF16) for o in _dot_many(ndb, ndb)]
        xs = left_products(xs, sq, 2, INV_BASE)
    n_blocks = c // INV_BASE
    if n_blocks == 1:
        return xs
    xbs = [x.astype(BF16) for x in xs]
    ms = [o.astype(BF16)
          for o in _dot_many(xbs, [jnp.where(same, 0.0, low).astype(BF16) for low in lows])]
    if n_blocks == 2:
        return [x - jnp.dot(m, xb, preferred_element_type=F32) for x, m, xb in zip(xs, ms, xbs)]
    out = [jnp.dot(m, jnp.concatenate([xb, m], axis=1), preferred_element_type=F32)
           for m, xb in zip(ms, xbs)]
    zs = [x - o[:, :c] for x, o in zip(xs, out)]
    m2 = [o[:, c:].astype(BF16) for o in out]
    return left_products(zs, m2, 2, n_blocks)


def _layer_prompt_kernel(x_ref, h0_ref, rc0_ref, s0_ref, gc0_ref,
                         wm_ref, bm_ref, wgt_ref, bgt_ref, wab_ref, bab_ref,
                         cw_ref, cb_ref, wg_ref, br_ref, bi_ref, lam_ref,
                         alog_ref, dtb_ref, nw_ref, wo_ref, g2_ref, b2_ref,
                         out_ref, h_ref, rc_ref, s_ref, gc_ref,
                         cbuf, h_sc, s_sc, *, tt, chunk, alpha, heads_per_pass):
    t = pl.program_id(1)
    nt = pl.num_programs(1)
    nc = tt // chunk

    @pl.when(t == 0)
    def _():
        cbuf[0:TAIL, :] = jnp.zeros((TAIL, CONV_CH), F32)
        for j in range(N_GROUPS):
            lo = j * LANES
            cbuf[HIST:TAIL, j * CONV_GW + P_RGX:j * CONV_GW + P_RGX + LANES] = rc0_ref[0, :, lo:lo + LANES]
            for c, p in enumerate((P_Q, P_K, P_V)):
                cbuf[HIST:TAIL, j * CONV_GW + p:j * CONV_GW + p + LANES] = (
                    gc0_ref[0, :, c * GDN_QK + lo:c * GDN_QK + lo + LANES])
        h_sc[...] = h0_ref[0]
        s_sc[...] = s0_ref[0]

    x = x_ref[0]
    xb = x.astype(BF16)
    ab_t = (jnp.dot(xb, wab_ref[...], preferred_element_type=F32) + bab_ref[...]).T
    g_t = -jnp.exp(alog_ref[...]) * _softplus(ab_t[0:GDN_HEADS] + dtb_ref[...])
    beta_t = jax.nn.sigmoid(ab_t[GDN_HEADS:2 * GDN_HEADS])
    lin = lax.broadcasted_iota(jnp.int32, (GDN_HEADS, tt), 1) % chunk
    d = 1
    while d < chunk:
        g_t = g_t + jnp.where(lin >= d, pltpu.roll(g_t, d, axis=1), 0.0)
        d *= 2
    cols = jnp.concatenate([g_t, beta_t, jnp.exp(g_t),
                            jnp.zeros((LANES - 3 * GDN_HEADS, tt), F32)], axis=0).T
    L_G, L_BETA, L_EG = 0, GDN_HEADS, 2 * GDN_HEADS
    c_lam_all = -RG_C * _softplus(-lam_ref[...])

    rw = lax.broadcasted_iota(jnp.int32, (chunk, chunk), 0)
    cl = lax.broadcasted_iota(jnp.int32, (chunk, chunk), 1)
    causal = rw >= cl
    strict = rw > cl
    sl = lambda arr, n: arr[n * chunk:(n + 1) * chunk]
    merged = []
    for j0 in range(0, N_GROUPS, heads_per_pass):
        heads = range(j0, j0 + heads_per_pass)

        pw = heads_per_pass * LANES

        def project(w_ref, b_ref, off):
            return (jnp.dot(xb, w_ref[:, off + j0 * LANES:off + j0 * LANES + pw], preferred_element_type=F32)
                    + b_ref[:, off + j0 * LANES:off + j0 * LANES + pw])

        pieces = {P_RGX: project(wm_ref, bm_ref, F_RGX), P_Q: project(wm_ref, bm_ref, F_Q),
                  P_K: project(wm_ref, bm_ref, F_K), P_V: project(wm_ref, bm_ref, F_V)}
        r_rgy = project(wm_ref, bm_ref, F_RGY)
        r_z = project(wm_ref, bm_ref, F_Z)
        r_ga = project(wgt_ref, bgt_ref, G_GA)
        r_gb = project(wgt_ref, bgt_ref, G_GB)
        gated_a, zgs, qs, ks, vs = {}, {}, {}, {}, {}
        for j in heads:
            lo = j * LANES
            cg = j * CONV_GW
            hl = slice((j - j0) * LANES, (j - j0 + 1) * LANES)
            for p, piece in pieces.items():
                cbuf[TAIL:TAIL + tt, cg + p:cg + p + LANES] = piece[:, hl]

            def conv(p):
                acc = cw_ref[CONV_W - 1:CONV_W, cg + p:cg + p + LANES] * pieces[p][:, hl]
                for i in range(CONV_W - 1):
                    acc = acc + (cw_ref[i:i + 1, cg + p:cg + p + LANES]
                                 * cbuf[HIST + i:HIST + i + tt, cg + p:cg + p + LANES])
                return acc

            xc = conv(P_RGX) + cb_ref[:, lo:lo + LANES]
            a, bt = _rg_gates(xc, wg_ref[j], br_ref[:, lo:lo + LANES], bi_ref[:, lo:lo + LANES],
                              c_lam_all[j:j + 1, :])
            hseq, h_last = _rg_scan(a, bt, h_sc[:, lo:lo + LANES])
            h_sc[:, lo:lo + LANES] = h_last
            gated_a[j] = jax.nn.sigmoid(r_ga[:, hl]) * (hseq * jax.nn.gelu(r_rgy[:, hl]))
            zgs[j] = jax.nn.sigmoid(r_gb[:, hl]) * jax.nn.silu(r_z[:, hl])
            qs[j] = _l2norm(jax.nn.silu(conv(P_Q))) * (GDN_DK ** -0.5)
            ks[j] = _l2norm(jax.nn.silu(conv(P_K)))
            vs[j] = jax.nn.silu(conv(P_V))

        c0, c1 = j0 * CONV_GW, (j0 + heads_per_pass) * CONV_GW
        cbuf[0:TAIL, c0:c1] = cbuf[tt:tt + TAIL, c0:c1]

        probs = [(j, n) for j in heads for n in range(nc)]
        gcol = {p: sl(cols, p[1])[:, L_G + p[0]:L_G + p[0] + 1] for p in probs}
        egc = {p: sl(cols, p[1])[:, L_EG + p[0]:L_EG + p[0] + 1] for p in probs}
        bcol = {p: sl(cols, p[1])[:, L_BETA + p[0]:L_BETA + p[0] + 1] for p in probs}
        decay = {p: jnp.exp(jnp.where(causal, gcol[p] - g_t[p[0]:p[0] + 1, p[1] * chunk:(p[1] + 1) * chunk],
                                      0.0)) for p in probs}
        kn = {p: sl(ks[p[0]], p[1]) for p in probs}
        qn = {p: sl(qs[p[0]], p[1]) for p in probs}
        kb = {p: kn[p] * bcol[p] for p in probs}
        kq = dict(zip(probs, _dot_nt_many(
            [jnp.concatenate([kb[p], qn[p]], axis=0).astype(BF16) for p in probs],
            [kn[p].astype(BF16) for p in probs])))
        lows = [jnp.where(strict, kq[p][:chunk] * decay[p], 0.0) for p in probs]
        a_intra = {p: jnp.where(causal, kq[p][chunk:] * decay[p], 0.0).astype(BF16) for p in probs}
        tinv = dict(zip(probs, _tri_inverse_many(lows, chunk)))
        sol = {p: jnp.dot(tinv[p].astype(BF16),
                          jnp.concatenate([sl(vs[p[0]], p[1]) * bcol[p], kb[p] * egc[p]], axis=1).astype(BF16),
                          preferred_element_type=F32) for p in probs}
        g_last = {p: gcol[p][chunk - 1:chunk, :] for p in probs}
        wq = {p: jnp.concatenate([sol[p][:, GDN_DV:], qn[p] * egc[p]], axis=0).astype(BF16) for p in probs}
        k_tail = {p: (kn[p] * jnp.exp(g_last[p] - gcol[p])).astype(BF16) for p in probs}

        states = {j: s_sc[j] for j in heads}
        outs = {}
        for n in range(nc):
            ws = dict(zip(heads, _dot_many([wq[(j, n)] for j in heads],
                                           [states[j].astype(BF16) for j in heads])))
            v_new = {j: (sol[(j, n)][:, :GDN_DV] - ws[j][:chunk]).astype(BF16) for j in heads}
            av = dict(zip(heads, _dot_many([a_intra[(j, n)] for j in heads], [v_new[j] for j in heads])))
            for j in heads:
                outs[(j, n)] = ws[j][chunk:] + av[j]
            states = {j: states[j] * jnp.exp(g_last[(j, n)])
                      + lax.dot_general(k_tail[(j, n)], v_new[j], (((0,), (0,)), ((), ())),
                                        preferred_element_type=F32) for j in heads}
        for j in heads:
            s_sc[j] = states[j]

        for j in heads:
            o = outs[(j, 0)] if nc == 1 else jnp.concatenate([outs[(j, n)] for n in range(nc)], axis=0)
            merged.append((gated_a[j] + _rms_gate(o, nw_ref[...], zgs[j])).astype(BF16))

    m = jnp.concatenate(merged, axis=1)
    y = jnp.dot(m, wo_ref[...], preferred_element_type=F32)
    out_ref[0] = _layer_norm(alpha * x + y, g2_ref[...], b2_ref[...])

    @pl.when(t == nt - 1)
    def _():
        h_ref[0] = h_sc[...]
        for j in range(N_GROUPS):
            lo = j * LANES
            rc_ref[0, :, lo:lo + LANES] = cbuf[HIST:TAIL, j * CONV_GW + P_RGX:j * CONV_GW + P_RGX + LANES]
            for c, p in enumerate((P_Q, P_K, P_V)):
                gc_ref[0, :, c * GDN_QK + lo:c * GDN_QK + lo + LANES] = (
                    cbuf[HIST:TAIL, j * CONV_GW + p:j * CONV_GW + p + LANES])
        s_ref[0] = s_sc[...]


def _layer_prompt(x, h0, rc0, s0, gc0, lp, *, layer, alpha, tt, chunk):
    bsz, seq, _ = x.shape
    nt = seq // tt
    cs = functools.partial(_layer_spec, layer)
    per_b3 = lambda b, t: (b, 0, 0)
    per_b4 = lambda b, t: (b, 0, 0, 0)
    kern = functools.partial(_layer_prompt_kernel, tt=tt, chunk=chunk, alpha=alpha,
                             heads_per_pass=N_GROUPS // 2)
    return pl.pallas_call(
        kern,
        grid=(bsz, nt),
        in_specs=[pl.BlockSpec((1, tt, D_MODEL), lambda b, t: (b, t, 0)),
                  pl.BlockSpec((1, 1, D_RNN), per_b3),
                  pl.BlockSpec((1, CONV_W - 1, D_RNN), per_b3),
                  pl.BlockSpec((1, GDN_HEADS, GDN_DK, GDN_DV), per_b4),
                  pl.BlockSpec((1, CONV_W - 1, 3 * GDN_QK), per_b3),
                  cs((D_MODEL, D_FRONT)), cs((1, D_FRONT)), cs((D_MODEL, D_GATE)), cs((1, D_GATE)),
                  cs((D_MODEL, LANES)), cs((1, LANES)),
                  cs((CONV_W, CONV_CH)), cs((1, D_RNN)),
                  cs((N_GROUPS, LANES, 2 * LANES)),
                  cs((1, D_RNN)), cs((1, D_RNN)), cs((N_GROUPS, LANES)),
                  cs((GDN_HEADS, 1)), cs((GDN_HEADS, 1)), cs((1, GDN_DV)),
                  cs((D_MODEL, D_MODEL)), cs((1, D_MODEL)), cs((1, D_MODEL))],
        out_specs=[pl.BlockSpec((1, tt, D_MODEL), lambda b, t: (b, t, 0)),
                   pl.BlockSpec((1, 1, D_RNN), per_b3),
                   pl.BlockSpec((1, CONV_W - 1, D_RNN), per_b3),
                   pl.BlockSpec((1, GDN_HEADS, GDN_DK, GDN_DV), per_b4),
                   pl.BlockSpec((1, CONV_W - 1, 3 * GDN_QK), per_b3)],
        out_shape=[jax.ShapeDtypeStruct((bsz, seq, D_MODEL), F32),
                   jax.ShapeDtypeStruct((bsz, 1, D_RNN), F32),
                   jax.ShapeDtypeStruct((bsz, CONV_W - 1, D_RNN), F32),
                   jax.ShapeDtypeStruct((bsz, GDN_HEADS, GDN_DK, GDN_DV), F32),
                   jax.ShapeDtypeStruct((bsz, CONV_W - 1, 3 * GDN_QK), F32)],
        scratch_shapes=[pltpu.VMEM((tt + TAIL, CONV_CH), F32),
                        pltpu.VMEM((1, D_RNN), F32),
                        pltpu.VMEM((GDN_HEADS, GDN_DK, GDN_DV), F32)],
        compiler_params=pltpu.CompilerParams(dimension_semantics=("arbitrary", "arbitrary"),
                                             vmem_limit_bytes=VMEM_LIMIT),
        name="mixer_prompt",
    )(x, h0, rc0, s0, gc0,
      lp["w_in_b"], lp["b_front"], lp["w_gate"], lp["b_gate"], lp["w_ab"], lp["b_ab"],
      lp["cw"], lp["cb"], lp["wg"], lp["br"], lp["bi"], lp["lam"],
      lp["alog_c"], lp["dtb_c"], lp["nw"], lp["w_o"], lp["ln2_g"], lp["ln2_b"])


def _mixer_sample_kernel(main_ref, ab_ref, h0_ref, rc0_ref, s0_ref, gc0_ref,
                         cw_ref, cb_ref, wg_ref, br_ref, bi_ref, lam_ref,
                         alog_ref, dtb_ref, nw_ref, *refs, bb, has_acc):
    merged_ref, h_ref, rc_ref, s_ref, gc_ref = refs[1:] if has_acc else refs
    nprev = CONV_W - 1

    def part(off, j):
        return main_ref[:, off + j * LANES:off + (j + 1) * LANES]

    def conv(hist_ref, ch, off_hist, j, p, off_main):
        cc = j * CONV_GW + p
        acc = cw_ref[nprev:CONV_W, cc:cc + LANES] * part(off_main, j)
        for i in range(nprev):
            acc = acc + (cw_ref[i:i + 1, cc:cc + LANES]
                         * hist_ref[:, i * ch + off_hist:i * ch + off_hist + LANES])
        return acc

    qkv_w = 3 * GDN_QK
    rc_ref[:, 0:2 * D_RNN] = rc0_ref[:, D_RNN:3 * D_RNN]
    rc_ref[:, 2 * D_RNN:3 * D_RNN] = main_ref[:, F_RGX:F_RGX + D_RNN]
    gc_ref[:, 0:2 * qkv_w] = gc0_ref[:, qkv_w:3 * qkv_w]
    gc_ref[:, 2 * qkv_w:3 * qkv_w] = main_ref[:, F_Q:F_Q + qkv_w]

    g, beta = _gdn_gates(ab_ref[...], alog_ref[...], dtb_ref[...])
    eg = jnp.exp(g)
    c_lam_all = -RG_C * _softplus(-lam_ref[...])

    for j in range(N_GROUPS):
        lo = j * LANES
        xc = conv(rc0_ref, D_RNN, lo, j, P_RGX, F_RGX) + cb_ref[:, lo:lo + LANES]
        a, bt = _rg_gates(xc, wg_ref[j], br_ref[:, lo:lo + LANES], bi_ref[:, lo:lo + LANES],
                          c_lam_all[j:j + 1, :])
        hseq = a * h0_ref[:, lo:lo + LANES] + bt
        h_ref[:, lo:lo + LANES] = hseq

        qn = _l2norm(jax.nn.silu(conv(gc0_ref, qkv_w, lo, j, P_Q, F_Q))) * (GDN_DK ** -0.5)
        kn = _l2norm(jax.nn.silu(conv(gc0_ref, qkv_w, GDN_QK + lo, j, P_K, F_K)))
        v = jax.nn.silu(conv(gc0_ref, qkv_w, 2 * GDN_QK + lo, j, P_V, F_V))
        qn_t = qn.T
        kn_t = kn.T
        o_rows = []
        for b in range(bb):
            s = s0_ref[b, j]
            kcol = kn_t[:, b:b + 1]
            egb = eg[b:b + 1, j:j + 1]
            ks = jnp.sum(s * kcol, axis=0, keepdims=True)
            v_new = beta[b:b + 1, GDN_HEADS + j:GDN_HEADS + j + 1] * (v[b:b + 1, :] - egb * ks)
            s_new = s * egb + kcol * v_new
            s_ref[b, j] = s_new
            o_rows.append(jnp.sum(s_new * qn_t[:, b:b + 1], axis=0, keepdims=True))
        o = jnp.concatenate(o_rows, axis=0)
        gated_a = jax.nn.sigmoid(part(D_FRONT + G_GA, j)) * (hseq * jax.nn.gelu(part(F_RGY, j)))
        zg = jax.nn.sigmoid(part(D_FRONT + G_GB, j)) * jax.nn.silu(part(F_Z, j))
        merged_ref[:, lo:lo + LANES] = (gated_a + _rms_gate(o, nw_ref[...], zg)).astype(BF16)


def _mixer_sample(main, ab, h0, rc0, s_all, gc0, lp, *, layer, s_acc, bb):
    bsz = main.shape[0]
    cs = functools.partial(_layer_spec, layer)
    row2 = lambda i: (i, 0)
    s_spec = pl.BlockSpec((None, bb, GDN_HEADS, GDN_DK, GDN_DV), lambda i: (layer, i, 0, 0, 0))
    has_acc = s_acc is not None
    kern = functools.partial(_mixer_sample_kernel, bb=bb, has_acc=has_acc)
    qkv3 = 3 * 3 * GDN_QK
    in_specs = [pl.BlockSpec((bb, D_MAIN), row2),
                pl.BlockSpec((bb, LANES), row2),
                pl.BlockSpec((bb, D_RNN), row2),
                pl.BlockSpec((bb, 3 * D_RNN), row2),
                s_spec,
                pl.BlockSpec((bb, qkv3), row2),
                cs((CONV_W, CONV_CH)), cs((1, D_RNN)),
                cs((N_GROUPS, LANES, 2 * LANES)),
                cs((1, D_RNN)), cs((1, D_RNN)), cs((N_GROUPS, LANES)),
                cs((1, LANES)), cs((1, LANES)), cs((1, GDN_DV))]
    args = [main, ab, h0, rc0, s_all, gc0, lp["cw"], lp["cb"], lp["wg"], lp["br"], lp["bi"], lp["lam"],
            lp["alog"], lp["dtb"], lp["nw"]]
    aliases = {}
    if has_acc:
        in_specs.append(pl.BlockSpec(memory_space=pl.ANY))
        args.append(s_acc)
        aliases = {len(args) - 1: 3}
    return pl.pallas_call(
        kern,
        grid=(bsz // bb,),
        in_specs=in_specs,
        out_specs=[pl.BlockSpec((bb, D_MODEL), row2),
                   pl.BlockSpec((bb, D_RNN), row2),
                   pl.BlockSpec((bb, 3 * D_RNN), row2),
                   s_spec,
                   pl.BlockSpec((bb, qkv3), row2)],
        out_shape=[jax.ShapeDtypeStruct((bsz, D_MODEL), BF16),
                   jax.ShapeDtypeStruct((bsz, D_RNN), F32),
                   jax.ShapeDtypeStruct((bsz, 3 * D_RNN), F32),
                   jax.ShapeDtypeStruct(s_all.shape, F32),
                   jax.ShapeDtypeStruct((bsz, qkv3), F32)],
        input_output_aliases=aliases,
        compiler_params=pltpu.CompilerParams(dimension_semantics=("arbitrary",),
                                             vmem_limit_bytes=VMEM_LIMIT),
        name="mixer_sample",
    )(*args)


def _prep_params(p):
    (ln1_g, ln1_b, ffn1_w1, ffn1_w3, ffn1_w2, w_in, b_in, rg_conv_w, rg_conv_b, rg_wr, rg_br,
     rg_wi, rg_bi, rg_lambda, gdn_conv_w, gdn_a_log, gdn_dt_bias, gdn_norm_w, w_o, ln2_g, ln2_b,
     ffn2_w1, ffn2_w3, ffn2_w2, ln3_g, ln3_b) = p
    depth = ln1_g.shape[0]
    row = lambda v: v.reshape(depth, 1, -1)

    def split(w):
        ab = jnp.pad(w[..., W_IN_AB:W_IN_GATES], [(0, 0)] * (w.ndim - 1) + [(0, LANES - 2 * GDN_HEADS)])
        return ab, w[..., W_IN_GATES:]

    w_in_b = w_in.astype(BF16)
    w_ab, w_gate = (t.astype(BF16) for t in split(w_in))
    b_ab, b_gate = split(row(b_in))

    def pair_blocks(w):
        w = w.reshape(depth * N_GROUPS, 2, RG_BLOCK, RG_BLOCK)
        z = jnp.zeros((depth * N_GROUPS, RG_BLOCK, RG_BLOCK), w.dtype)
        top = jnp.concatenate([w[:, 0], z], axis=2)
        bot = jnp.concatenate([z, w[:, 1]], axis=2)
        return jnp.concatenate([top, bot], axis=1).reshape(depth, N_GROUPS, LANES, LANES)

    wg = jnp.concatenate([pair_blocks(rg_wr), pair_blocks(rg_wi)], axis=3).astype(BF16)
    cw = jnp.concatenate([rg_conv_w.reshape(depth, CONV_W, N_GROUPS, 1, LANES),
                          gdn_conv_w.reshape(depth, CONV_W, 3, N_GROUPS, LANES).transpose(0, 1, 3, 2, 4)],
                         axis=3).reshape(depth, CONV_W, CONV_CH)
    pad_h = lambda v: jnp.pad(row(v), [(0, 0), (0, 0), (0, LANES - GDN_HEADS)])
    return dict(
        ln1_g=row(ln1_g), ln1_b=row(ln1_b), ln2_g=row(ln2_g), ln2_b=row(ln2_b),
        ln3_g=row(ln3_g), ln3_b=row(ln3_b),
        f1=(ffn1_w1.astype(BF16), ffn1_w3.astype(BF16), ffn1_w2.astype(BF16)),
        f2=(ffn2_w1.astype(BF16), ffn2_w3.astype(BF16), ffn2_w2.astype(BF16)),
        w_in_b=w_in_b, b_front=row(b_in)[..., :D_FRONT], w_gate=w_gate, b_gate=b_gate,
        w_ab=w_ab, b_ab=b_ab,
        w_o=w_o.astype(BF16),
        cw=cw, cb=row(rg_conv_b),
        wg=wg, br=row(rg_br), bi=row(rg_bi), lam=rg_lambda.reshape(depth, N_GROUPS, LANES),
        alog=pad_h(gdn_a_log), dtb=pad_h(gdn_dt_bias),
        alog_c=gdn_a_log.reshape(depth, GDN_HEADS, 1), dtb_c=gdn_dt_bias.reshape(depth, GDN_HEADS, 1),
        nw=row(gdn_norm_w))


def _trunk_prompt(x, lp, *, depth, alpha, tm, tt, chunk):
    bsz, seq, _ = x.shape
    h0 = jnp.zeros((bsz, 1, D_RNN), x.dtype)
    rc0 = jnp.zeros((bsz, CONV_W - 1, D_RNN), x.dtype)
    s0 = jnp.zeros((bsz, GDN_HEADS, GDN_DK, GDN_DV), x.dtype)
    gc0 = jnp.zeros((bsz, CONV_W - 1, 3 * GDN_QK), x.dtype)
    hs, rcs, ss, gcs = [], [], [], []
    for l in range(depth):
        x1 = _ffn_ln(x.reshape(bsz * seq, D_MODEL), *lp["f1"], lp["ln1_g"], lp["ln1_b"],
                     layer=l, alpha=alpha, tm=tm)
        x2, h, rc, s, gc = _layer_prompt(x1.reshape(bsz, seq, D_MODEL), h0, rc0, s0, gc0, lp,
                                         layer=l, alpha=alpha, tt=min(tt, seq), chunk=chunk)
        x = _ffn_ln(x2.reshape(bsz * seq, D_MODEL), *lp["f2"], lp["ln3_g"], lp["ln3_b"],
                    layer=l, alpha=alpha, tm=tm).reshape(bsz, seq, D_MODEL)
        hs.append(h.reshape(bsz, D_RNN))
        rcs.append(rc)
        ss.append(s)
        gcs.append(gc)
    return x, jnp.stack(hs), jnp.stack(rcs), jnp.stack(ss), jnp.stack(gcs)


def _trunk_sample(x, h0, rc0, s0, gc0, lp, *, depth, alpha, bb):
    bsz = x.shape[0]
    xf = x.reshape(bsz, D_MODEL)
    hs, rcs, gcs = [], [], []
    s_acc = None
    for l in range(depth):
        xf = _ffn_ln(xf, *lp["f1"], lp["ln1_g"], lp["ln1_b"], layer=l, alpha=alpha, tm=bsz)
        main, ab = _in_proj(xf, lp["w_in_b"], lp["b_front"], lp["w_gate"], lp["b_gate"], lp["w_ab"],
                            lp["b_ab"], layer=l, tm=bsz)
        merged, h, rc, s_acc, gc = _mixer_sample(
            main, ab, h0[l], rc0[l].reshape(bsz, -1), s0, gc0[l].reshape(bsz, -1), lp,
            layer=l, s_acc=s_acc, bb=bb)
        xf = _out_proj_ln(xf, merged, lp["w_o"], lp["ln2_g"], lp["ln2_b"], layer=l, alpha=alpha, tm=bsz)
        xf = _ffn_ln(xf, *lp["f2"], lp["ln3_g"], lp["ln3_b"], layer=l, alpha=alpha, tm=bsz)
        hs.append(h)
        rcs.append(rc.reshape(bsz, CONV_W - 1, D_RNN))
        gcs.append(gc.reshape(bsz, CONV_W - 1, 3 * GDN_QK))
    return xf.reshape(bsz, 1, D_MODEL), jnp.stack(hs), jnp.stack(rcs), s_acc, jnp.stack(gcs)


def kernel(x_prompt, x_sample, state_rglru_h, state_rglru_conv, state_gdn_S, state_gdn_conv, ln1_g, ln1_b, ffn1_w1, ffn1_w3, ffn1_w2, w_in, b_in, rg_conv_w, rg_conv_b, rg_wr, rg_br, rg_wi, rg_bi, rg_lambda, gdn_conv_w, gdn_a_log, gdn_dt_bias, gdn_norm_w, w_o, ln2_g, ln2_b, ffn2_w1, ffn2_w3, ffn2_w2, ln3_g, ln3_b):
    params = (ln1_g, ln1_b, ffn1_w1, ffn1_w3, ffn1_w2, w_in, b_in, rg_conv_w, rg_conv_b, rg_wr, rg_br,
              rg_wi, rg_bi, rg_lambda, gdn_conv_w, gdn_a_log, gdn_dt_bias, gdn_norm_w, w_o, ln2_g, ln2_b,
              ffn2_w1, ffn2_w3, ffn2_w2, ln3_g, ln3_b)
    depth = ln1_g.shape[0]
    lp = _prep_params(params)
    y_p, p_h, p_rc, p_s, p_gc = _trunk_prompt(x_prompt, lp, depth=depth, alpha=ALPHA, tm=512, tt=512,
                                              chunk=128)
    y_s, s_h, s_rc, s_s, s_gc = _trunk_sample(
        x_sample, state_rglru_h, state_rglru_conv, state_gdn_S, state_gdn_conv, lp,
        depth=depth, alpha=ALPHA, bb=8)
    return (y_p, y_s, p_h, p_rc, p_s, p_gc, s_h, s_rc, s_s, s_gc)
```

```python
import functools

import jax
import jax.numpy as jnp
from jax import lax
from jax.experimental import pallas as pl
from jax.experimental.pallas import tpu as pltpu

F32 = jnp.float32
BF16 = jnp.bfloat16

D_MODEL = 1024
DEPTH = 4
ALPHA = (2 * DEPTH) ** 0.25
D_RNN = D_MODEL
RG_BLOCK = 64
RG_C = 8.0
CONV_W = 4
GDN_HEADS = 8
GDN_DK = 128
GDN_DV = 128
GDN_QK = GDN_HEADS * GDN_DK
GDN_VW = GDN_HEADS * GDN_DV
D_FF = 2816
LN_EPS = 1e-5
RMS_EPS = 1e-6
LANES = 128
SUBLANES = 8
N_GROUPS = D_RNN // LANES

F_RGX, F_RGY, F_Q, F_K, F_V, F_Z = (i * D_RNN for i in range(6))
D_FRONT = 6 * D_RNN
W_IN_AB = D_FRONT
W_IN_GATES = D_FRONT + 2 * GDN_HEADS
G_GA, G_GB = 0, D_MODEL
D_GATE = 2 * D_MODEL
D_MAIN = D_FRONT + D_GATE
P_RGX, P_Q, P_K, P_V = (i * LANES for i in range(4))
CONV_GW = 4 * LANES
CONV_CH = N_GROUPS * CONV_GW
TAIL = SUBLANES
HIST = TAIL - (CONV_W - 1)
INV_BASE = 16
FF_CHUNK = 256

VMEM_LIMIT = 56 * 1024 * 1024


def _mm(a, b):
    return jnp.dot(a.astype(BF16), b.astype(BF16), preferred_element_type=F32)


def _softplus(x):
    return jnp.maximum(x, 0.0) + jnp.log1p(jnp.exp(-jnp.abs(x)))


def _layer_norm(r, g, b):
    mu = jnp.mean(r, axis=-1, keepdims=True)
    c = r - mu
    var = jnp.mean(c * c, axis=-1, keepdims=True)
    return c * lax.rsqrt(var + LN_EPS) * g + b


def _ffn_ln_math(x, w1_ref, w3_ref, w2_ref, g, b, *, alpha, ff_chunk):
    xb = x.astype(BF16)
    acc = None
    pending = None
    for c in range(D_FF // ff_chunk):
        sl = slice(c * ff_chunk, (c + 1) * ff_chunk)
        h1 = jnp.dot(xb, w1_ref[:, sl], preferred_element_type=F32)
        h3 = jnp.dot(xb, w3_ref[:, sl], preferred_element_type=F32)
        if pending is not None:
            y = jnp.dot(pending[0], w2_ref[pending[1], :], preferred_element_type=F32)
            acc = y if acc is None else acc + y
        pending = ((jax.nn.silu(h1) * h3).astype(BF16), sl)
    y = jnp.dot(pending[0], w2_ref[pending[1], :], preferred_element_type=F32)
    acc = y if acc is None else acc + y
    return _layer_norm(alpha * x + 0.5 * acc, g, b)


def _ffn_ln_kernel(x_ref, w1_ref, w3_ref, w2_ref, g_ref, b_ref, o_ref, *, alpha, ff_chunk):
    o_ref[...] = _ffn_ln_math(x_ref[...], w1_ref, w3_ref, w2_ref, g_ref[...], b_ref[...],
                              alpha=alpha, ff_chunk=ff_chunk)


def _layer_spec(layer, shape):
    nd = len(shape)
    return pl.BlockSpec((None,) + tuple(shape), lambda *_: (layer,) + (0,) * nd,
                        pipeline_mode=pl.Buffered(1))


def _ffn_ln(x, w1, w3, w2, g, b, *, layer, alpha, tm):
    m = x.shape[0]
    cs = functools.partial(_layer_spec, layer)
    return pl.pallas_call(
        functools.partial(_ffn_ln_kernel, alpha=alpha, ff_chunk=FF_CHUNK),
        grid=(m // tm,),
        in_specs=[pl.BlockSpec((tm, D_MODEL), lambda i: (i, 0)),
                  cs((D_MODEL, D_FF)), cs((D_MODEL, D_FF)),
                  cs((D_FF, D_MODEL)),
                  cs((1, D_MODEL)), cs((1, D_MODEL))],
        out_specs=pl.BlockSpec((tm, D_MODEL), lambda i: (i, 0)),
        out_shape=jax.ShapeDtypeStruct((m, D_MODEL), F32),
        compiler_params=pltpu.CompilerParams(dimension_semantics=("arbitrary",),
                                             vmem_limit_bytes=VMEM_LIMIT),
        name="ffn_ln",
    )(x, w1, w3, w2, g, b)


def _in_proj_kernel(x_ref, wf_ref, bf_ref, wgt_ref, bgt_ref, wab_ref, bab_ref, main_ref, ab_ref,
                    *, n_chunk):
    xb = x_ref[...].astype(BF16)
    for c0 in range(0, D_FRONT, n_chunk):
        main_ref[:, c0:c0 + n_chunk] = (
            jnp.dot(xb, wf_ref[:, c0:c0 + n_chunk], preferred_element_type=F32) + bf_ref[:, c0:c0 + n_chunk])
    for c0 in range(0, D_GATE, n_chunk):
        main_ref[:, D_FRONT + c0:D_FRONT + c0 + n_chunk] = (
            jnp.dot(xb, wgt_ref[:, c0:c0 + n_chunk], preferred_element_type=F32) + bgt_ref[:, c0:c0 + n_chunk])
    ab_ref[...] = jnp.dot(xb, wab_ref[...], preferred_element_type=F32) + bab_ref[...]


def _in_proj(x, w_in_b, b_front, w_gate, b_gate, w_ab, b_ab, *, layer, tm):
    m = x.shape[0]
    cs = functools.partial(_layer_spec, layer)
    return pl.pallas_call(
        functools.partial(_in_proj_kernel, n_chunk=D_RNN),
        grid=(m // tm,),
        in_specs=[pl.BlockSpec((tm, D_MODEL), lambda i: (i, 0)),
                  cs((D_MODEL, D_FRONT)), cs((1, D_FRONT)),
                  cs((D_MODEL, D_GATE)), cs((1, D_GATE)),
                  cs((D_MODEL, LANES)), cs((1, LANES))],
        out_specs=[pl.BlockSpec((tm, D_MAIN), lambda i: (i, 0)),
                   pl.BlockSpec((tm, LANES), lambda i: (i, 0))],
        out_shape=[jax.ShapeDtypeStruct((m, D_MAIN), F32),
                   jax.ShapeDtypeStruct((m, LANES), F32)],
        compiler_params=pltpu.CompilerParams(dimension_semantics=("arbitrary",),
                                             vmem_limit_bytes=VMEM_LIMIT),
        name="in_proj",
    )(x, w_in_b, b_front, w_gate, b_gate, w_ab, b_ab)


def _out_proj_ln_kernel(x_ref, m_ref, w_ref, g_ref, b_ref, o_ref, *, alpha):
    y = jnp.dot(m_ref[...], w_ref[...], preferred_element_type=F32)
    o_ref[...] = _layer_norm(alpha * x_ref[...] + y, g_ref[...], b_ref[...])


def _out_proj_ln(x, merged, w_o, g, b, *, layer, alpha, tm):
    m = x.shape[0]
    cs = functools.partial(_layer_spec, layer)
    return pl.pallas_call(
        functools.partial(_out_proj_ln_kernel, alpha=alpha),
        grid=(m // tm,),
        in_specs=[pl.BlockSpec((tm, D_MODEL), lambda i: (i, 0)),
                  pl.BlockSpec((tm, D_MODEL), lambda i: (i, 0)),
                  cs((D_MODEL, D_MODEL)),
                  cs((1, D_MODEL)), cs((1, D_MODEL))],
        out_specs=pl.BlockSpec((tm, D_MODEL), lambda i: (i, 0)),
        out_shape=jax.ShapeDtypeStruct((m, D_MODEL), F32),
        compiler_params=pltpu.CompilerParams(dimension_semantics=("arbitrary",),
                                             vmem_limit_bytes=VMEM_LIMIT),
        name="out_proj_ln",
    )(x, merged, w_o, g, b)


def _rg_gates(xc, wg, br, bi, c_lam):
    pre = _mm(xc, wg)
    r = jax.nn.sigmoid(pre[:, :LANES] + br)
    i = jax.nn.sigmoid(pre[:, LANES:] + bi)
    log_a = c_lam * r
    a = jnp.exp(log_a)
    bt = jnp.sqrt(1.0 - a * a) * (i * xc)
    return a, bt


def _l2norm(t):
    return t * lax.rsqrt(jnp.sum(t * t, axis=-1, keepdims=True) + RMS_EPS)


def _rms_gate(o, nw, zg):
    return o * lax.rsqrt(jnp.mean(o * o, axis=-1, keepdims=True) + RMS_EPS) * nw * zg


def _gdn_gates(ab, alog, dtb):
    g = -jnp.exp(alog) * _softplus(ab + dtb)
    beta = jax.nn.sigmoid(ab)
    return g, beta


def _rg_scan(a, b, h_prev):
    n = a.shape[0]
    nv = n // SUBLANES
    a3 = a.reshape(nv, SUBLANES, LANES)
    b3 = b.reshape(nv, SUBLANES, LANES)
    sub = lax.broadcasted_iota(jnp.int32, (nv, SUBLANES, LANES), 1)
    d = 1
    while d < SUBLANES:
        keep = sub >= d
        a_sh = jnp.where(keep, pltpu.roll(a3, d, axis=1), 1.0)
        b_sh = jnp.where(keep, pltpu.roll(b3, d, axis=1), 0.0)
        b3 = a3 * b_sh + b3
        a3 = a3 * a_sh
        d *= 2
    carry = h_prev
    hs = []
    for v in range(nv):
        hv = a3[v] * carry + b3[v]
        hs.append(hv)
        carry = hv[SUBLANES - 1:SUBLANES, :]
    return jnp.concatenate(hs, axis=0), carry


def _dot_many(xs, ys):
    return [jnp.dot(x, y, preferred_element_type=F32) for x, y in zip(xs, ys)]


def _dot_nt_many(xs, ys):
    return [lax.dot_general(x, y, (((1,), (1,)), ((), ())), preferred_element_type=F32)
            for x, y in zip(xs, ys)]


def _tri_inverse_many(lows, c):
    row = lax.broadcasted_iota(jnp.int32, (c, c), 0)
    col = lax.broadcasted_iota(jnp.int32, (c, c), 1)
    eye = (row == col).astype(F32)
    same = (row // INV_BASE) == (col // INV_BASE)

    def left_products(acc, pw, exp, limit):
        while exp < limit:
            if 2 * exp < limit:
                rhs = [jnp.concatenate([p, a.astype(BF16)], axis=1) for p, a in zip(pw, acc)]
                out = [jnp.dot(p, r, preferred_element_type=F32) for p, r in zip(pw, rhs)]
                pw = [o[:, :c].astype(BF16) for o in out]
                acc = [a + o[:, c:] for a, o in zip(acc, out)]
            else:
                acc = [a + o for a, o in zip(acc, _dot_many(pw, [a.astype(BF16) for a in acc]))]
            exp *= 2
        return acc

    nd = [(-jnp.where(same, low, 0.0)) for low in lows]
    ndb = [x.astype(BF16) for x in nd]
    xs = [eye + x for x in nd]
    if INV_BASE > 2:
        sq = [o.astype(BF16) for o in _dot_many(ndb, ndb)]
        xs = left_products(xs, sq, 2, INV_BASE)
    n_blocks = c // INV_BASE
    if n_blocks == 1:
        return xs
    xbs = [x.astype(BF16) for x in xs]
    ms = [o.astype(BF16)
          for o in _dot_many(xbs, [jnp.where(same, 0.0, low).astype(BF16) for low in lows])]
    if n_blocks == 2:
        return [x - jnp.dot(m, xb, preferred_element_type=F32) for x, m, xb in zip(xs, ms, xbs)]
    out = [jnp.dot(m, jnp.concatenate([xb, m], axis=1), preferred_element_type=F32)
           for m, xb in zip(ms, xbs)]
    zs = [x - o[:, :c] for x, o in zip(xs, out)]
    m2 = [o[:, c:].astype(BF16) for o in out]
    return left_products(zs, m2, 2, n_blocks)


def _layer_prompt_kernel(x_ref, h0_ref, rc0_ref, s0_ref, gc0_ref,
                         wm_ref, bm_ref, wgt_ref, bgt_ref, wab_ref, bab_ref,
                         cw_ref, cb_ref, wg_ref, br_ref, bi_ref, lam_ref,
                         alog_ref, dtb_ref, nw_ref, wo_ref, g2_ref, b2_ref,
                         out_ref, h_ref, rc_ref, s_ref, gc_ref,
                         cbuf, h_sc, s_sc, *, tt, chunk, alpha, heads_per_pass):
    t = pl.program_id(1)
    nt = pl.num_programs(1)
    nc = tt // chunk

    @pl.when(t == 0)
    def _():
        cbuf[0:TAIL, :] = jnp.zeros((TAIL, CONV_CH), F32)
        for j in range(N_GROUPS):
            lo = j * LANES
            cbuf[HIST:TAIL, j * CONV_GW + P_RGX:j * CONV_GW + P_RGX + LANES] = rc0_ref[0, :, lo:lo + LANES]
            for c, p in enumerate((P_Q, P_K, P_V)):
                cbuf[HIST:TAIL, j * CONV_GW + p:j * CONV_GW + p + LANES] = (
                    gc0_ref[0, :, c * GDN_QK + lo:c * GDN_QK + lo + LANES])
        h_sc[...] = h0_ref[0]
        s_sc[...] = s0_ref[0]

    x = x_ref[0]
    xb = x.astype(BF16)
    ab_t = (jnp.dot(xb, wab_ref[...], preferred_element_type=F32) + bab_ref[...]).T
    g_t = -jnp.exp(alog_ref[...]) * _softplus(ab_t[0:GDN_HEADS] + dtb_ref[...])
    beta_t = jax.nn.sigmoid(ab_t[GDN_HEADS:2 * GDN_HEADS])
    lin = lax.broadcasted_iota(jnp.int32, (GDN_HEADS, tt), 1) % chunk
    d = 1
    while d < chunk:
        g_t = g_t + jnp.where(lin >= d, pltpu.roll(g_t, d, axis=1), 0.0)
        d *= 2
    cols = jnp.concatenate([g_t, beta_t, jnp.exp(g_t),
                            jnp.zeros((LANES - 3 * GDN_HEADS, tt), F32)], axis=0).T
    L_G, L_BETA, L_EG = 0, GDN_HEADS, 2 * GDN_HEADS
    c_lam_all = -RG_C * _softplus(-lam_ref[...])

    rw = lax.broadcasted_iota(jnp.int32, (chunk, chunk), 0)
    cl = lax.broadcasted_iota(jnp.int32, (chunk, chunk), 1)
    causal = rw >= cl
    strict = rw > cl
    sl = lambda arr, n: arr[n * chunk:(n + 1) * chunk]
    merged = []
    for j0 in range(0, N_GROUPS, heads_per_pass):
        heads = range(j0, j0 + heads_per_pass)

        pw = heads_per_pass * LANES

        def project(w_ref, b_ref, off):
            return (jnp.dot(xb, w_ref[:, off + j0 * LANES:off + j0 * LANES + pw], preferred_element_type=F32)
                    + b_ref[:, off + j0 * LANES:off + j0 * LANES + pw])

        pieces = {P_RGX: project(wm_ref, bm_ref, F_RGX), P_Q: project(wm_ref, bm_ref, F_Q),
                  P_K: project(wm_ref, bm_ref, F_K), P_V: project(wm_ref, bm_ref, F_V)}
        r_rgy = project(wm_ref, bm_ref, F_RGY)
        r_z = project(wm_ref, bm_ref, F_Z)
        r_ga = project(wgt_ref, bgt_ref, G_GA)
        r_gb = project(wgt_ref, bgt_ref, G_GB)
        gated_a, zgs, qs, ks, vs = {}, {}, {}, {}, {}
        for j in heads:
            lo = j * LANES
            cg = j * CONV_GW
            hl = slice((j - j0) * LANES, (j - j0 + 1) * LANES)
            for p, piece in pieces.items():
                cbuf[TAIL:TAIL + tt, cg + p:cg + p + LANES] = piece[:, hl]

            def conv(p):
                acc = cw_ref[CONV_W - 1:CONV_W, cg + p:cg + p + LANES] * pieces[p][:, hl]
                for i in range(CONV_W - 1):
                    acc = acc + (cw_ref[i:i + 1, cg + p:cg + p + LANES]
                                 * cbuf[HIST + i:HIST + i + tt, cg + p:cg + p + LANES])
                return acc

            xc = conv(P_RGX) + cb_ref[:, lo:lo + LANES]
            a, bt = _rg_gates(xc, wg_ref[j], br_ref[:, lo:lo + LANES], bi_ref[:, lo:lo + LANES],
                              c_lam_all[j:j + 1, :])
            hseq, h_last = _rg_scan(a, bt, h_sc[:, lo:lo + LANES])
            h_sc[:, lo:lo + LANES] = h_last
            gated_a[j] = jax.nn.sigmoid(r_ga[:, hl]) * (hseq * jax.nn.gelu(r_rgy[:, hl]))
            zgs[j] = jax.nn.sigmoid(r_gb[:, hl]) * jax.nn.silu(r_z[:, hl])
            qs[j] = _l2norm(jax.nn.silu(conv(P_Q))) * (GDN_DK ** -0.5)
            ks[j] = _l2norm(jax.nn.silu(conv(P_K)))
            vs[j] = jax.nn.silu(conv(P_V))

        c0, c1 = j0 * CONV_GW, (j0 + heads_per_pass) * CONV_GW
        cbuf[0:TAIL, c0:c1] = cbuf[tt:tt + TAIL, c0:c1]

        probs = [(j, n) for j in heads for n in range(nc)]
        gcol = {p: sl(cols, p[1])[:, L_G + p[0]:L_G + p[0] + 1] for p in probs}
        egc = {p: sl(cols, p[1])[:, L_EG + p[0]:L_EG + p[0] + 1] for p in probs}
        bcol = {p: sl(cols, p[1])[:, L_BETA + p[0]:L_BETA + p[0] + 1] for p in probs}
        decay = {p: jnp.exp(jnp.where(causal, gcol[p] - g_t[p[0]:p[0] + 1, p[1] * chunk:(p[1] + 1) * chunk],
                                      0.0)) for p in probs}
        kn = {p: sl(ks[p[0]], p[1]) for p in probs}
        qn = {p: sl(qs[p[0]], p[1]) for p in probs}
        kb = {p: kn[p] * bcol[p] for p in probs}
        kq = dict(zip(probs, _dot_nt_many(
            [jnp.concatenate([kb[p], qn[p]], axis=0).astype(BF16) for p in probs],
            [kn[p].astype(BF16) for p in probs])))
        lows = [jnp.where(strict, kq[p][:chunk] * decay[p], 0.0) for p in probs]
        a_intra = {p: jnp.where(causal, kq[p][chunk:] * decay[p], 0.0).astype(BF16) for p in probs}
        tinv = dict(zip(probs, _tri_inverse_many(lows, chunk)))
        sol = {p: jnp.dot(tinv[p].astype(BF16),
                          jnp.concatenate([sl(vs[p[0]], p[1]) * bcol[p], kb[p] * egc[p]], axis=1).astype(BF16),
                          preferred_element_type=F32) for p in probs}
        g_last = {p: gcol[p][chunk - 1:chunk, :] for p in probs}
        wq = {p: jnp.concatenate([sol[p][:, GDN_DV:], qn[p] * egc[p]], axis=0).astype(BF16) for p in probs}
        k_tail = {p: (kn[p] * jnp.exp(g_last[p] - gcol[p])).astype(BF16) for p in probs}

        states = {j: s_sc[j] for j in heads}
        outs = {}
        for n in range(nc):
            ws = dict(zip(heads, _dot_many([wq[(j, n)] for j in heads],
                                           [states[j].astype(BF16) for j in heads])))
            v_new = {j: (sol[(j, n)][:, :GDN_DV] - ws[j][:chunk]).astype(BF16) for j in heads}
            av = dict(zip(heads, _dot_many([a_intra[(j, n)] for j in heads], [v_new[j] for j in heads])))
            for j in heads:
                outs[(j, n)] = ws[j][chunk:] + av[j]
            states = {j: states[j] * jnp.exp(g_last[(j, n)])
                      + lax.dot_general(k_tail[(j, n)], v_new[j], (((0,), (0,)), ((), ())),
                                        preferred_element_type=F32) for j in heads}
        for j in heads:
            s_sc[j] = states[j]

        for j in heads:
            o = outs[(j, 0)] if nc == 1 else jnp.concatenate([outs[(j, n)] for n in range(nc)], axis=0)
            merged.append((gated_a[j] + _rms_gate(o, nw_ref[...], zgs[j])).astype(BF16))

    m = jnp.concatenate(merged, axis=1)
    y = jnp.dot(m, wo_ref[...], preferred_element_type=F32)
    out_ref[0] = _layer_norm(alpha * x + y, g2_ref[...], b2_ref[...])

    @pl.when(t == nt - 1)
    def _():
        h_ref[0] = h_sc[...]
        for j in range(N_GROUPS):
            lo = j * LANES
            rc_ref[0, :, lo:lo + LANES] = cbuf[HIST:TAIL, j * CONV_GW + P_RGX:j * CONV_GW + P_RGX + LANES]
            for c, p in enumerate((P_Q, P_K, P_V)):
                gc_ref[0, :, c * GDN_QK + lo:c * GDN_QK + lo + LANES] = (
                    cbuf[HIST:TAIL, j * CONV_GW + p:j * CONV_GW + p + LANES])
        s_ref[0] = s_sc[...]


def _layer_prompt(x, h0, rc0, s0, gc0, lp, *, layer, alpha, tt, chunk):
    bsz, seq, _ = x.shape
    nt = seq // tt
    cs = functools.partial(_layer_spec, layer)
    per_b3 = lambda b, t: (b, 0, 0)
    per_b4 = lambda b, t: (b, 0, 0, 0)
    kern = functools.partial(_layer_prompt_kernel, tt=tt, chunk=chunk, alpha=alpha,
                             heads_per_pass=N_GROUPS // 2)
    return pl.pallas_call(
        kern,
        grid=(bsz, nt),
        in_specs=[pl.BlockSpec((1, tt, D_MODEL), lambda b, t: (b, t, 0)),
                  pl.BlockSpec((1, 1, D_RNN), per_b3),
                  pl.BlockSpec((1, CONV_W - 1, D_RNN), per_b3),
                  pl.BlockSpec((1, GDN_HEADS, GDN_DK, GDN_DV), per_b4),
                  pl.BlockSpec((1, CONV_W - 1, 3 * GDN_QK), per_b3),
                  cs((D_MODEL, D_FRONT)), cs((1, D_FRONT)), cs((D_MODEL, D_GATE)), cs((1, D_GATE)),
                  cs((D_MODEL, LANES)), cs((1, LANES)),
                  cs((CONV_W, CONV_CH)), cs((1, D_RNN)),
                  cs((N_GROUPS, LANES, 2 * LANES)),
                  cs((1, D_RNN)), cs((1, D_RNN)), cs((N_GROUPS, LANES)),
                  cs((GDN_HEADS, 1)), cs((GDN_HEADS, 1)), cs((1, GDN_DV)),
                  cs((D_MODEL, D_MODEL)), cs((1, D_MODEL)), cs((1, D_MODEL))],
        out_specs=[pl.BlockSpec((1, tt, D_MODEL), lambda b, t: (b, t, 0)),
                   pl.BlockSpec((1, 1, D_RNN), per_b3),
                   pl.BlockSpec((1, CONV_W - 1, D_RNN), per_b3),
                   pl.BlockSpec((1, GDN_HEADS, GDN_DK, GDN_DV), per_b4),
                   pl.BlockSpec((1, CONV_W - 1, 3 * GDN_QK), per_b3)],
        out_shape=[jax.ShapeDtypeStruct((bsz, seq, D_MODEL), F32),
                   jax.ShapeDtypeStruct((bsz, 1, D_RNN), F32),
                   jax.ShapeDtypeStruct((bsz, CONV_W - 1, D_RNN), F32),
                   jax.ShapeDtypeStruct((bsz, GDN_HEADS, GDN_DK, GDN_DV), F32),
                   jax.ShapeDtypeStruct((bsz, CONV_W - 1, 3 * GDN_QK), F32)],
        scratch_shapes=[pltpu.VMEM((tt + TAIL, CONV_CH), F32),
                        pltpu.VMEM((1, D_RNN), F32),
                        pltpu.VMEM((GDN_HEADS, GDN_DK, GDN_DV), F32)],
        compiler_params=pltpu.CompilerParams(dimension_semantics=("arbitrary", "arbitrary"),
                                             vmem_limit_bytes=VMEM_LIMIT),
        name="mixer_prompt",
    )(x, h0, rc0, s0, gc0,
      lp["w_in_b"], lp["b_front"], lp["w_gate"], lp["b_gate"], lp["w_ab"], lp["b_ab"],
      lp["cw"], lp["cb"], lp["wg"], lp["br"], lp["bi"], lp["lam"],
      lp["alog_c"], lp["dtb_c"], lp["nw"], lp["w_o"], lp["ln2_g"], lp["ln2_b"])


def _mixer_sample_kernel(main_ref, ab_ref, h0_ref, rc0_ref, s0_ref, gc0_ref,
                         cw_ref, cb_ref, wg_ref, br_ref, bi_ref, lam_ref,
                         alog_ref, dtb_ref, nw_ref, *refs, bb, has_acc):
    merged_ref, h_ref, rc_ref, s_ref, gc_ref = refs[1:] if has_acc else refs
    nprev = CONV_W - 1

    def part(off, j):
        return main_ref[:, off + j * LANES:off + (j + 1) * LANES]

    def conv(hist_ref, ch, off_hist, j, p, off_main):
        cc = j * CONV_GW + p
        acc = cw_ref[nprev:CONV_W, cc:cc + LANES] * part(off_main, j)
        for i in range(nprev):
            acc = acc + (cw_ref[i:i + 1, cc:cc + LANES]
                         * hist_ref[:, i * ch + off_hist:i * ch + off_hist + LANES])
        return acc

    qkv_w = 3 * GDN_QK
    rc_ref[:, 0:2 * D_RNN] = rc0_ref[:, D_RNN:3 * D_RNN]
    rc_ref[:, 2 * D_RNN:3 * D_RNN] = main_ref[:, F_RGX:F_RGX + D_RNN]
    gc_ref[:, 0:2 * qkv_w] = gc0_ref[:, qkv_w:3 * qkv_w]
    gc_ref[:, 2 * qkv_w:3 * qkv_w] = main_ref[:, F_Q:F_Q + qkv_w]

    g, beta = _gdn_gates(ab_ref[...], alog_ref[...], dtb_ref[...])
    eg = jnp.exp(g)
    c_lam_all = -RG_C * _softplus(-lam_ref[...])

    for j in range(N_GROUPS):
        lo = j * LANES
        xc = conv(rc0_ref, D_RNN, lo, j, P_RGX, F_RGX) + cb_ref[:, lo:lo + LANES]
        a, bt = _rg_gates(xc, wg_ref[j], br_ref[:, lo:lo + LANES], bi_ref[:, lo:lo + LANES],
                          c_lam_all[j:j + 1, :])
        hseq = a * h0_ref[:, lo:lo + LANES] + bt
        h_ref[:, lo:lo + LANES] = hseq

        qn = _l2norm(jax.nn.silu(conv(gc0_ref, qkv_w, lo, j, P_Q, F_Q))) * (GDN_DK ** -0.5)
        kn = _l2norm(jax.nn.silu(conv(gc0_ref, qkv_w, GDN_QK + lo, j, P_K, F_K)))
        v = jax.nn.silu(conv(gc0_ref, qkv_w, 2 * GDN_QK + lo, j, P_V, F_V))
        qn_t = qn.T
        kn_t = kn.T
        o_rows = []
        for b in range(bb):
            s = s0_ref[b, j]
            kcol = kn_t[:, b:b + 1]
            egb = eg[b:b + 1, j:j + 1]
            ks = jnp.sum(s * kcol, axis=0, keepdims=True)
            v_new = beta[b:b + 1, GDN_HEADS + j:GDN_HEADS + j + 1] * (v[b:b + 1, :] - egb * ks)
            s_new = s * egb + kcol * v_new
            s_ref[b, j] = s_new
            o_rows.append(jnp.sum(s_new * qn_t[:, b:b + 1], axis=0, keepdims=True))
        o = jnp.concatenate(o_rows, axis=0)
        gated_a = jax.nn.sigmoid(part(D_FRONT + G_GA, j)) * (hseq * jax.nn.gelu(part(F_RGY, j)))
        zg = jax.nn.sigmoid(part(D_FRONT + G_GB, j)) * jax.nn.silu(part(F_Z, j))
        merged_ref[:, lo:lo + LANES] = (gated_a + _rms_gate(o, nw_ref[...], zg)).astype(BF16)


def _mixer_sample(main, ab, h0, rc0, s_all, gc0, lp, *, layer, s_acc, bb):
    bsz = main.shape[0]
    cs = functools.partial(_layer_spec, layer)
    row2 = lambda i: (i, 0)
    s_spec = pl.BlockSpec((None, bb, GDN_HEADS, GDN_DK, GDN_DV), lambda i: (layer, i, 0, 0, 0))
    has_acc = s_acc is not None
    kern = functools.partial(_mixer_sample_kernel, bb=bb, has_acc=has_acc)
    qkv3 = 3 * 3 * GDN_QK
    in_specs = [pl.BlockSpec((bb, D_MAIN), row2),
                pl.BlockSpec((bb, LANES), row2),
                pl.BlockSpec((bb, D_RNN), row2),
                pl.BlockSpec((bb, 3 * D_RNN), row2),
                s_spec,
                pl.BlockSpec((bb, qkv3), row2),
                cs((CONV_W, CONV_CH)), cs((1, D_RNN)),
                cs((N_GROUPS, LANES, 2 * LANES)),
                cs((1, D_RNN)), cs((1, D_RNN)), cs((N_GROUPS, LANES)),
                cs((1, LANES)), cs((1, LANES)), cs((1, GDN_DV))]
    args = [main, ab, h0, rc0, s_all, gc0, lp["cw"], lp["cb"], lp["wg"], lp["br"], lp["bi"], lp["lam"],
            lp["alog"], lp["dtb"], lp["nw"]]
    aliases = {}
    if has_acc:
        in_specs.append(pl.BlockSpec(memory_space=pl.ANY))
        args.append(s_acc)
        aliases = {len(args) - 1: 3}
    return pl.pallas_call(
        kern,
        grid=(bsz // bb,),
        in_specs=in_specs,
        out_specs=[pl.BlockSpec((bb, D_MODEL), row2),
                   pl.BlockSpec((bb, D_RNN), row2),
                   pl.BlockSpec((bb, 3 * D_RNN), row2),
                   s_spec,
                   pl.BlockSpec((bb, qkv3), row2)],
        out_shape=[jax.ShapeDtypeStruct((bsz, D_MODEL), BF16),
                   jax.ShapeDtypeStruct((bsz, D_RNN), F32),
                   jax.ShapeDtypeStruct((bsz, 3 * D_RNN), F32),
                   jax.ShapeDtypeStruct(s_all.shape, F32),
                   jax.ShapeDtypeStruct((bsz, qkv3), F32)],
        input_output_aliases=aliases,
        compiler_params=pltpu.CompilerParams(dimension_semantics=("arbitrary",),
                                             vmem_limit_bytes=VMEM_LIMIT),
        name="mixer_sample",
    )(*args)


def _prep_params(p):
    (ln1_g, ln1_b, ffn1_w1, ffn1_w3, ffn1_w2, w_in, b_in, rg_conv_w, rg_conv_b, rg_wr, rg_br,
     rg_wi, rg_bi, rg_lambda, gdn_conv_w, gdn_a_log, gdn_dt_bias, gdn_norm_w, w_o, ln2_g, ln2_b,
     ffn2_w1, ffn2_w3, ffn2_w2, ln3_g, ln3_b) = p
    depth = ln1_g.shape[0]
    row = lambda v: v.reshape(depth, 1, -1)

    def split(w):
        ab = jnp.pad(w[..., W_IN_AB:W_IN_GATES], [(0, 0)] * (w.ndim - 1) + [(0, LANES - 2 * GDN_HEADS)])
        return ab, w[..., W_IN_GATES:]

    w_in_b = w_in.astype(BF16)
    w_ab, w_gate = (t.astype(BF16) for t in split(w_in))
    b_ab, b_gate = split(row(b_in))

    def pair_blocks(w):
        w = w.reshape(depth * N_GROUPS, 2, RG_BLOCK, RG_BLOCK)
        z = jnp.zeros((depth * N_GROUPS, RG_BLOCK, RG_BLOCK), w.dtype)
        top = jnp.concatenate([w[:, 0], z], axis=2)
        bot = jnp.concatenate([z, w[:, 1]], axis=2)
        return jnp.concatenate([top, bot], axis=1).reshape(depth, N_GROUPS, LANES, LANES)

    wg = jnp.concatenate([pair_blocks(rg_wr), pair_blocks(rg_wi)], axis=3).astype(BF16)
    cw = jnp.concatenate([rg_conv_w.reshape(depth, CONV_W, N_GROUPS, 1, LANES),
                          gdn_conv_w.reshape(depth, CONV_W, 3, N_GROUPS, LANES).transpose(0, 1, 3, 2, 4)],
                         axis=3).reshape(depth, CONV_W, CONV_CH)
    pad_h = lambda v: jnp.pad(row(v), [(0, 0), (0, 0), (0, LANES - GDN_HEADS)])
    return dict(
        ln1_g=row(ln1_g), ln1_b=row(ln1_b), ln2_g=row(ln2_g), ln2_b=row(ln2_b),
        ln3_g=row(ln3_g), ln3_b=row(ln3_b),
        f1=(ffn1_w1.astype(BF16), ffn1_w3.astype(BF16), ffn1_w2.astype(BF16)),
        f2=(ffn2_w1.astype(BF16), ffn2_w3.astype(BF16), ffn2_w2.astype(BF16)),
        w_in_b=w_in_b, b_front=row(b_in)[..., :D_FRONT], w_gate=w_gate, b_gate=b_gate,
        w_ab=w_ab, b_ab=b_ab,
        w_o=w_o.astype(BF16),
        cw=cw, cb=row(rg_conv_b),
        wg=wg, br=row(rg_br), bi=row(rg_bi), lam=rg_lambda.reshape(depth, N_GROUPS, LANES),
        alog=pad_h(gdn_a_log), dtb=pad_h(gdn_dt_bias),
        alog_c=gdn_a_log.reshape(depth, GDN_HEADS, 1), dtb_c=gdn_dt_bias.reshape(depth, GDN_HEADS, 1),
        nw=row(gdn_norm_w))


def _trunk_prompt(x, lp, *, depth, alpha, tm, tt, chunk):
    bsz, seq, _ = x.shape
    h0 = jnp.zeros((bsz, 1, D_RNN), x.dtype)
    rc0 = jnp.zeros((bsz, CONV_W - 1, D_RNN), x.dtype)
    s0 = jnp.zeros((bsz, GDN_HEADS, GDN_DK, GDN_DV), x.dtype)
    gc0 = jnp.zeros((bsz, CONV_W - 1, 3 * GDN_QK), x.dtype)
    hs, rcs, ss, gcs = [], [], [], []
    for l in range(depth):
        x1 = _ffn_ln(x.reshape(bsz * seq, D_MODEL), *lp["f1"], lp["ln1_g"], lp["ln1_b"],
                     layer=l, alpha=alpha, tm=tm)
        x2, h, rc, s, gc = _layer_prompt(x1.reshape(bsz, seq, D_MODEL), h0, rc0, s0, gc0, lp,
                                         layer=l, alpha=alpha, tt=min(tt, seq), chunk=chunk)
        x = _ffn_ln(x2.reshape(bsz * seq, D_MODEL), *lp["f2"], lp["ln3_g"], lp["ln3_b"],
                    layer=l, alpha=alpha, tm=tm).reshape(bsz, seq, D_MODEL)
        hs.append(h.reshape(bsz, D_RNN))
        rcs.append(rc)
        ss.append(s)
        gcs.append(gc)
    return x, jnp.stack(hs), jnp.stack(rcs), jnp.stack(ss), jnp.stack(gcs)


def _trunk_sample(x, h0, rc0, s0, gc0, lp, *, depth, alpha, bb):
    bsz = x.shape[0]
    xf = x.reshape(bsz, D_MODEL)
    hs, rcs, gcs = [], [], []
    s_acc = None
    for l in range(depth):
        xf = _ffn_ln(xf, *lp["f1"], lp["ln1_g"], lp["ln1_b"], layer=l, alpha=alpha, tm=bsz)
        main, ab = _in_proj(xf, lp["w_in_b"], lp["b_front"], lp["w_gate"], lp["b_gate"], lp["w_ab"],
                            lp["b_ab"], layer=l, tm=bsz)
        merged, h, rc, s_acc, gc = _mixer_sample(
            main, ab, h0[l], rc0[l].reshape(bsz, -1), s0, gc0[l].reshape(bsz, -1), lp,
            layer=l, s_acc=s_acc, bb=bb)
        xf = _out_proj_ln(xf, merged, lp["w_o"], lp["ln2_g"], lp["ln2_b"], layer=l, alpha=alpha, tm=bsz)
        xf = _ffn_ln(xf, *lp["f2"], lp["ln3_g"], lp["ln3_b"], layer=l, alpha=alpha, tm=bsz)
        hs.append(h)
        rcs.append(rc.reshape(bsz, CONV_W - 1, D_RNN))
        gcs.append(gc.reshape(bsz, CONV_W - 1, 3 * GDN_QK))
    return xf.reshape(bsz, 1, D_MODEL), jnp.stack(hs), jnp.stack(rcs), s_acc, jnp.stack(gcs)


def kernel(x_prompt, x_sample, state_rglru_h, state_rglru_conv, state_gdn_S, state_gdn_conv, ln1_g, ln1_b, ffn1_w1, ffn1_w3, ffn1_w2, w_in, b_in, rg_conv_w, rg_conv_b, rg_wr, rg_br, rg_wi, rg_bi, rg_lambda, gdn_conv_w, gdn_a_log, gdn_dt_bias, gdn_norm_w, w_o, ln2_g, ln2_b, ffn2_w1, ffn2_w3, ffn2_w2, ln3_g, ln3_b):
    params = (ln1_g, ln1_b, ffn1_w1, ffn1_w3, ffn1_w2, w_in, b_in, rg_conv_w, rg_conv_b, rg_wr, rg_br,
              rg_wi, rg_bi, rg_lambda, gdn_conv_w, gdn_a_log, gdn_dt_bias, gdn_norm_w, w_o, ln2_g, ln2_b,
              ffn2_w1, ffn2_w3, ffn2_w2, ln3_g, ln3_b)
    depth = ln1_g.shape[0]
    lp = _prep_params(params)
    y_p, p_h, p_rc, p_s, p_gc = _trunk_prompt(x_prompt, lp, depth=depth, alpha=ALPHA, tm=1024, tt=512,
                                              chunk=128)
    y_s, s_h, s_rc, s_s, s_gc = _trunk_sample(
        x_sample, state_rglru_h, state_rglru_conv, state_gdn_S, state_gdn_conv, lp,
        depth=depth, alpha=ALPHA, bb=8)
    return (y_p, y_s, p_h, p_rc, p_s, p_gc, s_h, s_rc, s_s, s_gc)
```
